```python
import jax, jax.numpy as jnp
from jax import lax
import numpy as np

D_MODEL = 1024
BATCH = 32
SEQ = 2048
DEPTH = 2
DEC_BATCH = 32
DEC_SEQ = 32
PAST_LEN = 2048

CHUNK = 64
A_CHUNK = 128
A_WIDTH = D_MODEL // 2
A_GROUPS = 4
A_GROUP_DIM = A_WIDTH // A_GROUPS
B_HEAD_DIM = 64
B_HEADS = D_MODEL // 128
B_WIDTH = B_HEADS * B_HEAD_DIM
B_PREV_CHUNKS = 8
B_REACH = B_PREV_CHUNKS * CHUNK
REL_CLIP = 128
C_WIDTH = D_MODEL // 2
CONV_WIDTH = 3
N_BRANCH = 3
BRANCH_WIDTH = D_MODEL // 2
IN_COLS = 2 * A_WIDTH + 3 * B_WIDTH + 3 * C_WIDTH + N_BRANCH * D_MODEL
D_FF = ((8 * D_MODEL // 3 + 255) // 256) * 256
N_EXPERTS = 8
TOP_K = 2
D_FF_EXPERT = 7 * D_MODEL // 2
EPS = 1e-6
NEG_INF = -1e30

kernel_name = 'hybrid_stream_encoder_step'


def rmsnorm(x, g):
    x32 = x.astype(jnp.float32)
    y = x32 * lax.rsqrt(jnp.mean(x32 * x32, axis=-1, keepdims=True) + EPS)
    return (y * g.astype(jnp.float32)).astype(x.dtype)


def project(h, w_in):
    z = jnp.einsum('btd,dc->btc', h, w_in)
    sizes = [A_WIDTH, A_WIDTH, B_WIDTH, B_WIDTH, B_WIDTH, C_WIDTH, C_WIDTH, C_WIDTH]
    offsets = [int(o) for o in np.cumsum(sizes)]
    return jnp.split(z, offsets, axis=-1)


def gmlp_mix(u, v, ws, bs):
    b, L, _ = u.shape
    lc = min(L, A_CHUNK)
    n = L // lc
    pos = np.arange(lc)
    mask = (pos[None, :] // CHUNK) <= (pos[:, None] // CHUNK)
    w = jnp.where(mask[None], ws[:, :lc, :lc], 0)
    v5 = v.reshape(b, n, lc, A_GROUPS, A_GROUP_DIM)
    mixed = jnp.einsum('gij,bnjgd->bnigd', w, v5) + jnp.transpose(bs[:, :lc])[None, None, :, :, None]
    return u * mixed.reshape(b, L, A_WIDTH)


def rel_bias(table, n_q, n_k, offset):
    dist = offset + jnp.arange(n_q)[:, None] - jnp.arange(n_k)[None, :]
    idx = jnp.clip(dist, -REL_CLIP, REL_CLIP) + REL_CLIP
    return jnp.take(table, idx, axis=1).astype(jnp.float32)


def attend(q, k, v, bias, valid=None):
    s = jnp.einsum('bqhd,bkhd->bhqk', q, k).astype(jnp.float32) * (B_HEAD_DIM ** -0.5) + bias[None]
    if valid is not None:
        s = jnp.where(valid, s, NEG_INF)
    p = jax.nn.softmax(s, axis=-1)
    return jnp.einsum('bhqk,bkhd->bqhd', p.astype(v.dtype), v)


def band_attention(q, k, v, table):
    b, s, hh, dh = q.shape
    nc = s // CHUNK
    band = B_REACH + CHUNK
    kp = jnp.pad(k, ((0, 0), (B_REACH, 0), (0, 0), (0, 0)))
    vp = jnp.pad(v, ((0, 0), (B_REACH, 0), (0, 0), (0, 0)))
    bias = rel_bias(table, CHUNK, band, B_REACH)
    qc = jnp.moveaxis(q.reshape(b, nc, CHUNK, hh, dh), 1, 0)

    def one_chunk(args):
        c, qb = args
        start = c * CHUNK
        kb = lax.dynamic_slice_in_dim(kp, start, band, axis=1)
        vb = lax.dynamic_slice_in_dim(vp, start, band, axis=1)
        valid = (start - B_REACH + jnp.arange(band)) >= 0
        return attend(qb, kb, vb, bias, valid[None, None, None, :])

    out = lax.map(one_chunk, (jnp.arange(nc), qc))
    return jnp.moveaxis(out, 0, 1).reshape(b, s, hh, dh)


def causal_conv(zp, w, L):
    y = zp[:, 0:L] * w[0]
    for t in range(1, CONV_WIDTH):
        y = y + zp[:, t:t + L] * w[t]
    return y


def merge(a, bo, c, gates, w_branch, w_out):
    b, L, _ = a.shape
    br = jnp.stack([a, bo, c], axis=-2)
    p = jnp.einsum('blnw,nwd->blnd', br, w_branch)
    g = jax.nn.sigmoid(gates.reshape(b, L, N_BRANCH, D_MODEL))
    m = jnp.sum(g * p, axis=-2)
    return jnp.einsum('bld,de->ble', m, w_out)


def mixer_prompt(h, w_in, g_v, ws, bs, table, conv_w, w_branch, w_out):
    b, s, _ = h.shape
    ua, va, q, k, v, cin, gb, gc, gates = project(h, w_in)
    vn = rmsnorm(jax.nn.gelu(va), g_v)
    a_out = gmlp_mix(jax.nn.gelu(ua), vn, ws, bs)
    q = q.reshape(b, s, B_HEADS, B_HEAD_DIM)
    k = k.reshape(b, s, B_HEADS, B_HEAD_DIM)
    v = v.reshape(b, s, B_HEADS, B_HEAD_DIM)
    b_out = band_attention(q, k, v, table).reshape(b, s, B_WIDTH)
    zp = jnp.pad(gc * cin, ((0, 0), (CONV_WIDTH - 1, 0), (0, 0)))
    c_out = gb * causal_conv(zp, conv_w, s)
    y = merge(a_out, b_out, c_out, gates, w_branch, w_out)
    keep = min(B_REACH, s)
    return y, k[:, s - keep:], v[:, s - keep:], zp[:, -(CONV_WIDTH - 1):]


def mixer_sample(h, ck, cv, cs, w_in, g_v, ws, bs, table, conv_w, w_branch, w_out):
    b, t, _ = h.shape
    ua, va, q, k, v, cin, gb, gc, gates = project(h, w_in)
    vn = rmsnorm(jax.nn.gelu(va), g_v)
    a_out = gmlp_mix(jax.nn.gelu(ua), vn, ws, bs)
    q = q.reshape(b, t, B_HEADS, B_HEAD_DIM)
    k = k.reshape(b, t, B_HEADS, B_HEAD_DIM)
    v = v.reshape(b, t, B_HEADS, B_HEAD_DIM)
    keep = ck.shape[1]
    kk = jnp.concatenate([ck, k], axis=1)
    vv = jnp.concatenate([cv, v], axis=1)
    bias = rel_bias(table, t, keep + t, keep)
    b_out = attend(q, kk, vv, bias).reshape(b, t, B_WIDTH)
    zp = jnp.concatenate([cs, gc * cin], axis=1)
    c_out = gb * causal_conv(zp, conv_w, t)
    y = merge(a_out, b_out, c_out, gates, w_branch, w_out)
    return y, k, v, zp[:, -(CONV_WIDTH - 1):], vn


def swiglu(h, w1, w3, w2):
    a = jnp.einsum('...d,df->...f', h, w1)
    g = jnp.einsum('...d,df->...f', h, w3)
    return jnp.einsum('...f,fd->...d', jax.nn.silu(a) * g, w2)


def moe(h, w_router, b_router, w1, w3, w2):
    logits = jnp.einsum('...d,de->...e', h, w_router).astype(jnp.float32) + b_router.astype(jnp.float32)
    top_v, top_i = lax.top_k(logits, TOP_K)
    top_w = jax.nn.softmax(top_v, axis=-1)
    combine = jnp.einsum('...k,...ke->...e', top_w, jax.nn.one_hot(top_i, N_EXPERTS, dtype=jnp.float32))
    y = None
    for e in range(N_EXPERTS):
        term = combine[..., e:e + 1].astype(h.dtype) * swiglu(h, w1[e], w3[e], w2[e])
        y = term if y is None else y + term
    return y


def setup_inputs(seed: int = 0) -> dict:
    key = jax.random.key(seed)
    ks = jax.random.split(key, 24)
    nrm = jax.random.normal
    f32 = jnp.float32
    d = D_MODEL
    keep = min(B_REACH, PAST_LEN)
    n_dense = (DEPTH + 1) // 2
    n_moe = DEPTH // 2
    return {
        'x_prompt': nrm(ks[0], (BATCH, SEQ, d), f32),
        'x_sample': nrm(ks[1], (DEC_BATCH, DEC_SEQ, d), f32),
        'cache_attn_k': nrm(ks[2], (DEPTH, DEC_BATCH, keep, B_HEADS, B_HEAD_DIM), f32),
        'cache_attn_v': nrm(ks[3], (DEPTH, DEC_BATCH, keep, B_HEADS, B_HEAD_DIM), f32),
        'state_conv': nrm(ks[4], (DEPTH, DEC_BATCH, CONV_WIDTH - 1, C_WIDTH), f32),
        'norm_mix_g': 1.0 + 0.05 * nrm(ks[5], (DEPTH, d), f32),
        'w_in': nrm(ks[6], (DEPTH, d, IN_COLS), f32) * d ** -0.5,
        'gmlp_norm_g': 1.0 + 0.05 * nrm(ks[7], (DEPTH, A_WIDTH), f32),
        'gmlp_ws': nrm(ks[8], (DEPTH, A_GROUPS, A_CHUNK, A_CHUNK), f32) * (0.5 * A_CHUNK ** -0.5),
        'gmlp_bs': 1.0 + 0.1 * nrm(ks[9], (DEPTH, A_GROUPS, A_CHUNK), f32),
        'attn_rel_bias': 0.5 * nrm(ks[10], (DEPTH, B_HEADS, 2 * REL_CLIP + 1), f32),
        'conv_w': nrm(ks[11], (DEPTH, CONV_WIDTH, C_WIDTH), f32) * CONV_WIDTH ** -0.5,
        'w_branch': nrm(ks[12], (DEPTH, N_BRANCH, BRANCH_WIDTH, d), f32) * BRANCH_WIDTH ** -0.5,
        'w_out': nrm(ks[13], (DEPTH, d, d), f32) * d ** -0.5,
        'norm_ffn_g': 1.0 + 0.05 * nrm(ks[14], (DEPTH, d), f32),
        'ffn_w1': nrm(ks[15], (n_dense, d, D_FF), f32) * d ** -0.5,
        'ffn_w3': nrm(ks[16], (n_dense, d, D_FF), f32) * d ** -0.5,
        'ffn_w2': nrm(ks[17], (n_dense, D_FF, d), f32) * D_FF ** -0.5,
        'moe_router': nrm(ks[18], (n_moe, d, N_EXPERTS), f32) * d ** -0.5,
        'moe_router_b': 0.01 * nrm(ks[19], (n_moe, N_EXPERTS), f32),
        'moe_w1': nrm(ks[20], (n_moe, N_EXPERTS, d, D_FF_EXPERT), f32) * d ** -0.5,
        'moe_w3': nrm(ks[21], (n_moe, N_EXPERTS, d, D_FF_EXPERT), f32) * d ** -0.5,
        'moe_w2': nrm(ks[22], (n_moe, N_EXPERTS, D_FF_EXPERT, d), f32) * D_FF_EXPERT ** -0.5,
        'final_norm_g': 1.0 + 0.05 * nrm(ks[23], (d,), f32),
    }


def reference(x_prompt, x_sample, cache_attn_k, cache_attn_v, state_conv, norm_mix_g, w_in,
              gmlp_norm_g, gmlp_ws, gmlp_bs, attn_rel_bias, conv_w, w_branch, w_out, norm_ffn_g,
              ffn_w1, ffn_w3, ffn_w2, moe_router, moe_router_b, moe_w1, moe_w3, moe_w2, final_norm_g):
    xp, xs = x_prompt, x_sample
    pk, pv, pc, sk, sv, sc, sg = [], [], [], [], [], [], []
    for l in range(DEPTH):
        yp, k_p, v_p, c_p = mixer_prompt(rmsnorm(xp, norm_mix_g[l]), w_in[l], gmlp_norm_g[l], gmlp_ws[l],
                                         gmlp_bs[l], attn_rel_bias[l], conv_w[l], w_branch[l], w_out[l])
        ys, k_s, v_s, c_s, g_s = mixer_sample(rmsnorm(xs, norm_mix_g[l]), cache_attn_k[l], cache_attn_v[l],
                                              state_conv[l], w_in[l], gmlp_norm_g[l], gmlp_ws[l], gmlp_bs[l],
                                              attn_rel_bias[l], conv_w[l], w_branch[l], w_out[l])
        xp = xp + yp
        xs = xs + ys
        hp = rmsnorm(xp, norm_ffn_g[l])
        hs = rmsnorm(xs, norm_ffn_g[l])
        i = l // 2
        if l % 2 == 0:
            xp = xp + swiglu(hp, ffn_w1[i], ffn_w3[i], ffn_w2[i])
            xs = xs + swiglu(hs, ffn_w1[i], ffn_w3[i], ffn_w2[i])
        else:
            xp = xp + moe(hp, moe_router[i], moe_router_b[i], moe_w1[i], moe_w3[i], moe_w2[i])
            xs = xs + moe(hs, moe_router[i], moe_router_b[i], moe_w1[i], moe_w3[i], moe_w2[i])
        pk.append(k_p); pv.append(v_p); pc.append(c_p)
        sk.append(k_s); sv.append(v_s); sc.append(c_s); sg.append(g_s)
    y_prompt = rmsnorm(xp, final_norm_g)
    y_sample = rmsnorm(xs, final_norm_g)
    prompt_attn_k = jnp.stack(pk)
    prompt_attn_v = jnp.stack(pv)
    prompt_conv_state = jnp.stack(pc)
    sample_attn_k = jnp.stack(sk)
    sample_attn_v = jnp.stack(sv)
    sample_conv_state = jnp.stack(sc)
    sample_gmlp_v = jnp.stack(sg)
    return (y_prompt, y_sample, prompt_attn_k, prompt_attn_v, prompt_conv_state,
            sample_attn_k, sample_attn_v, sample_conv_state, sample_gmlp_v)
```

```python
import functools

import jax
import jax.numpy as jnp
import numpy as np
from jax import lax
from jax.experimental import pallas as pl
from jax.experimental.pallas import tpu as pltpu

F32 = jnp.float32
BF16 = jnp.bfloat16

D_MODEL = 1024
DEPTH = 2
CHUNK = 64
A_CHUNK = 128
A_WIDTH = 512
A_GROUPS = 4
A_GROUP_DIM = 128
HEAD_DIM = 64
HEADS = 8
B_WIDTH = 512
REACH = 512
REL_CLIP = 128
C_WIDTH = 512
CONV_WIDTH = 3
N_BRANCH = 3
D_FF = 2816
N_EXPERTS = 8
D_FF_EXPERT = 3584
EPS = 1e-6
NEG_INF = -1e30

_O_UA, _O_VA, _O_Q, _O_K, _O_V, _O_CIN, _O_GB, _O_GC, _O_GATES = (
    0, 512, 1024, 1536, 2048, 2560, 3072, 3584, 4096)
IN_COLS = 7168

TOKEN_TILE = 512
VMEM_LIMIT = 56 * 1024 * 1024
LANES = 128
ROUTER_PAD = 128


def _params(*sem):
    return pltpu.CompilerParams(dimension_semantics=sem, vmem_limit_bytes=VMEM_LIMIT)


def _const_spec(shape):
    nd = len(shape)
    return pl.BlockSpec(shape, lambda *_: (0,) * nd, pipeline_mode=pl.Buffered(1))


def _rms(x, g):
    ms = jnp.mean(x * x, axis=-1, keepdims=True)
    return x * lax.rsqrt(ms + EPS) * g


def _inproj_kernel(x_ref, g_ref, w_ref, gv_ref,
                   ug_ref, vn_ref, q_ref, k_ref, v_ref, zc_ref, gb_ref, sg_ref):
    h = _rms(x_ref[...], g_ref[...]).astype(BF16)

    def col(a, n=512):
        return jnp.dot(h, w_ref[:, a:a + n], preferred_element_type=F32)

    ug_ref[...] = jax.nn.gelu(col(_O_UA)).astype(BF16)
    vn_ref[...] = _rms(jax.nn.gelu(col(_O_VA)), gv_ref[...])
    q_ref[...] = (col(_O_Q) * (HEAD_DIM ** -0.5)).astype(BF16)
    k_ref[...] = col(_O_K)
    v_ref[...] = col(_O_V)
    zc_ref[...] = col(_O_GC) * col(_O_CIN)
    gb_ref[...] = col(_O_GB).astype(BF16)
    for j in range(N_BRANCH * D_MODEL // 512):
        sg_ref[:, j * 512:(j + 1) * 512] = jax.nn.sigmoid(col(_O_GATES + j * 512)).astype(BF16)


def _inproj(x, g, w_in, g_v):
    t = x.shape[0]
    tm = TOKEN_TILE
    row = lambda n: pl.BlockSpec((tm, n), lambda i: (i, 0))
    outs = [
        jax.ShapeDtypeStruct((t, A_WIDTH), BF16),
        jax.ShapeDtypeStruct((t, A_WIDTH), F32),
        jax.ShapeDtypeStruct((t, B_WIDTH), BF16),
        jax.ShapeDtypeStruct((t, B_WIDTH), F32),
        jax.ShapeDtypeStruct((t, B_WIDTH), F32),
        jax.ShapeDtypeStruct((t, C_WIDTH), F32),
        jax.ShapeDtypeStruct((t, C_WIDTH), BF16),
        jax.ShapeDtypeStruct((t, N_BRANCH * D_MODEL), BF16),
    ]
    return pl.pallas_call(
        _inproj_kernel,
        grid=(t // tm,),
        in_specs=[row(D_MODEL), _const_spec((1, D_MODEL)), _const_spec((D_MODEL, IN_COLS)),
                  _const_spec((1, A_WIDTH))],
        out_specs=[row(A_WIDTH), row(A_WIDTH), row(B_WIDTH), row(B_WIDTH), row(B_WIDTH),
                   row(C_WIDTH), row(C_WIDTH), row(N_BRANCH * D_MODEL)],
        out_shape=outs,
        compiler_params=_params("parallel"),
        name="inproj",
    )(x, g, w_in, g_v)


def _attend_rows(q_ref, kk_ref, vv_ref, bias_ref, bo_ref, q0, k0, nq, nk, valid):
    lane = lax.broadcasted_iota(jnp.int32, (nq, LANES), 1)
    for p in range(HEADS // 2):
        cs = slice(p * LANES, (p + 1) * LANES)
        qp = q_ref[pl.ds(q0, nq), cs]
        kp = kk_ref[pl.ds(k0, nk), cs]
        vp = vv_ref[pl.ds(k0, nk), cs]
        outs = []
        for hh in range(2):
            sel = (lane < HEAD_DIM) if hh == 0 else (lane >= HEAD_DIM)
            qh = jnp.where(sel, qp, jnp.zeros_like(qp))
            s = lax.dot_general(qh, kp, (((1,), (1,)), ((), ())), preferred_element_type=F32)
            s = s + bias_ref[2 * p + hh]
            if valid is not None:
                s = jnp.where(valid, s, NEG_INF)
            m = jnp.max(s, axis=-1, keepdims=True)
            e = jnp.exp(s - m)
            l = jnp.sum(e, axis=-1, keepdims=True)
            outs.append(jnp.dot(e.astype(BF16), vp, preferred_element_type=F32) / l)
        o = jnp.where(lane < HEAD_DIM, outs[0], outs[1])
        bo_ref[pl.ds(q0, nq), cs] = o.astype(BF16)


def _conv_rows(zext_ref, gb_ref, cw_ref, co_ref, r0, n, pad):
    z0 = zext_ref[pl.ds(pad + r0, n), :]
    z1 = zext_ref[pl.ds(pad + r0 - 1, n), :]
    z2 = zext_ref[pl.ds(pad + r0 - 2, n), :]
    y = z2 * cw_ref[0:1, :] + z1 * cw_ref[1:2, :] + z0 * cw_ref[2:3, :]
    co_ref[pl.ds(r0, n), :] = (gb_ref[pl.ds(r0, n), :].astype(F32) * y).astype(BF16)


_ZPAD = 8


def _seq_prompt_kernel(q_ref, k_ref, v_ref, zc_ref, gb_ref, bias_ref, cw_ref,
                       bo_ref, co_ref, kpad, vpad, zext):
    s = q_ref.shape[0]
    kpad[0:REACH, :] = jnp.zeros((REACH, B_WIDTH), BF16)
    vpad[0:REACH, :] = jnp.zeros((REACH, B_WIDTH), BF16)
    kpad[REACH:, :] = k_ref[...].astype(BF16)
    vpad[REACH:, :] = v_ref[...].astype(BF16)
    zext[0:_ZPAD, :] = jnp.zeros((_ZPAD, C_WIDTH), F32)
    zext[_ZPAD:, :] = zc_ref[...]

    band = REACH + CHUNK

    def chunk(c, carry):
        r0 = pl.multiple_of(c * CHUNK, CHUNK)
        kpos = lax.broadcasted_iota(jnp.int32, (CHUNK, band), 1) + (r0 - REACH)
        _attend_rows(q_ref, kpad, vpad, bias_ref, bo_ref, r0, r0, CHUNK, band, kpos >= 0)
        return carry

    lax.fori_loop(0, s // CHUNK, chunk, 0)

    conv_rows = 256
    for i in range(s // conv_rows):
        _conv_rows(zext, gb_ref, cw_ref, co_ref, i * conv_rows, conv_rows, _ZPAD)


def _seq_prompt(q, k, v, zc, gb, bias, cw, nb, s):
    seq = lambda n: pl.BlockSpec((s, n), lambda b: (b, 0))
    band = REACH + CHUNK
    return pl.pallas_call(
        _seq_prompt_kernel,
        grid=(nb,),
        in_specs=[seq(B_WIDTH), seq(B_WIDTH), seq(B_WIDTH), seq(C_WIDTH), seq(C_WIDTH),
                  _const_spec((HEADS, CHUNK, band)), _const_spec((CONV_WIDTH, C_WIDTH))],
        out_specs=[seq(B_WIDTH), seq(C_WIDTH)],
        out_shape=[jax.ShapeDtypeStruct((nb * s, B_WIDTH), BF16),
                   jax.ShapeDtypeStruct((nb * s, C_WIDTH), BF16)],
        scratch_shapes=[pltpu.VMEM((REACH + s, B_WIDTH), BF16),
                        pltpu.VMEM((REACH + s, B_WIDTH), BF16),
                        pltpu.VMEM((_ZPAD + s, C_WIDTH), F32)],
        compiler_params=_params("parallel"),
        name="seq_prompt",
    )(q, k, v, zc, gb, bias, cw)


def _seq_sample_kernel(q_ref, k_ref, v_ref, ck_ref, cv_ref, zc_ref, gb_ref, cs_ref, bias_ref, cw_ref,
                       bo_ref, co_ref, ns_ref, kk, vv, zext):
    t = q_ref.shape[0]
    keep = ck_ref.shape[0]
    kk[0:keep, :] = ck_ref[...].astype(BF16)
    vv[0:keep, :] = cv_ref[...].astype(BF16)
    kk[keep:, :] = k_ref[...].astype(BF16)
    vv[keep:, :] = v_ref[...].astype(BF16)
    hist = CONV_WIDTH - 1
    zext[_ZPAD - hist:_ZPAD, :] = cs_ref[...]
    zext[_ZPAD:, :] = zc_ref[...]
    _attend_rows(q_ref, kk, vv, bias_ref, bo_ref, 0, 0, t, keep + t, None)
    _conv_rows(zext, gb_ref, cw_ref, co_ref, 0, t, _ZPAD)
    ns_ref[...] = zext[_ZPAD + t - hist:_ZPAD + t, :]


def _seq_sample(q, k, v, ck, cv, zc, gb, cs, bias, cw, nb, t):
    keep = ck.shape[1]
    hist = CONV_WIDTH - 1
    seq = lambda n: pl.BlockSpec((t, n), lambda b: (b, 0))
    per_b = lambda r, n: pl.BlockSpec((None, r, n), lambda b: (b, 0, 0))
    return pl.pallas_call(
        _seq_sample_kernel,
        grid=(nb,),
        in_specs=[seq(B_WIDTH), seq(B_WIDTH), seq(B_WIDTH), per_b(keep, B_WIDTH), per_b(keep, B_WIDTH),
                  seq(C_WIDTH), seq(C_WIDTH), per_b(hist, C_WIDTH),
                  _const_spec((HEADS, t, keep + t)), _const_spec((CONV_WIDTH, C_WIDTH))],
        out_specs=[seq(B_WIDTH), seq(C_WIDTH), per_b(hist, C_WIDTH)],
        out_shape=[jax.ShapeDtypeStruct((nb * t, B_WIDTH), BF16),
                   jax.ShapeDtypeStruct((nb * t, C_WIDTH), BF16),
                   jax.ShapeDtypeStruct((nb, hist, C_WIDTH), F32)],
        scratch_shapes=[pltpu.VMEM((keep + t, B_WIDTH), BF16),
                        pltpu.VMEM((keep + t, B_WIDTH), BF16),
                        pltpu.VMEM((_ZPAD + t, C_WIDTH), F32)],
        compiler_params=_params("parallel"),
        name="seq_sample",
    )(q, k, v, ck, cv, zc, gb, cs, bias, cw)


def _merge_kernel(ug_ref, vn_ref, bo_ref, co_ref, sg_ref, x_ref, wm_ref, bsb_ref, wb_ref, wo_ref,
                  o_ref, a_s):
    tm = ug_ref.shape[0]
    for n in range(tm // A_CHUNK):
        rows = slice(n * A_CHUNK, (n + 1) * A_CHUNK)
        for g in range(A_GROUPS):
            cols = slice(g * A_GROUP_DIM, (g + 1) * A_GROUP_DIM)
            mixed = jnp.dot(wm_ref[g], vn_ref[rows, cols].astype(BF16),
                            preferred_element_type=F32) + bsb_ref[g]
            a_s[rows, cols] = (ug_ref[rows, cols].astype(F32) * mixed).astype(BF16)
    m = None
    for n, br in enumerate((a_s, bo_ref, co_ref)):
        p = jnp.dot(br[...], wb_ref[n], preferred_element_type=F32)
        term = sg_ref[:, n * D_MODEL:(n + 1) * D_MODEL].astype(F32) * p
        m = term if m is None else m + term
    y = jnp.dot(m.astype(BF16), wo_ref[...], preferred_element_type=F32)
    o_ref[...] = x_ref[...] + y


def _merge(ug, vn, bo, co, sg, x, wm, bsb, wb, wo):
    t = x.shape[0]
    tm = TOKEN_TILE
    row = lambda n: pl.BlockSpec((tm, n), lambda i: (i, 0))
    return pl.pallas_call(
        _merge_kernel,
        grid=(t // tm,),
        in_specs=[row(A_WIDTH), row(A_WIDTH), row(B_WIDTH), row(C_WIDTH), row(N_BRANCH * D_MODEL),
                  row(D_MODEL),
                  _const_spec((A_GROUPS, A_CHUNK, A_CHUNK)), _const_spec((A_GROUPS, A_CHUNK, A_GROUP_DIM)),
                  _const_spec((N_BRANCH, A_WIDTH, D_MODEL)), _const_spec((D_MODEL, D_MODEL))],
        out_specs=row(D_MODEL),
        out_shape=jax.ShapeDtypeStruct((t, D_MODEL), F32),
        scratch_shapes=[pltpu.VMEM((tm, A_WIDTH), BF16)],
        compiler_params=_params("parallel"),
        name="merge",
    )(ug, vn, bo, co, sg, x, wm, bsb, wb, wo)


_FF_SPLITS = ((0, 1280), (1280, 1536))


def _ffn_kernel(x_ref, g_ref, w1_ref, w3_ref, w2_ref, o_ref):
    x = x_ref[...]
    h = _rms(x, g_ref[...]).astype(BF16)
    y = x
    for a, n in _FF_SPLITS:
        u = jnp.dot(h, w1_ref[:, a:a + n], preferred_element_type=F32)
        g = jnp.dot(h, w3_ref[:, a:a + n], preferred_element_type=F32)
        mid = (jax.nn.silu(u) * g).astype(BF16)
        y = y + jnp.dot(mid, w2_ref[a:a + n, :], preferred_element_type=F32)
    o_ref[...] = y


def _ffn(x, g, w1, w3, w2):
    t = x.shape[0]
    tm = TOKEN_TILE
    row = pl.BlockSpec((tm, D_MODEL), lambda i: (i, 0))
    return pl.pallas_call(
        _ffn_kernel,
        grid=(t // tm,),
        in_specs=[row, _const_spec((1, D_MODEL)), _const_spec((D_MODEL, D_FF)),
                  _const_spec((D_MODEL, D_FF)), _const_spec((D_FF, D_MODEL))],
        out_specs=row,
        out_shape=jax.ShapeDtypeStruct((t, D_MODEL), F32),
        compiler_params=_params("parallel"),
        name="ffn",
    )(x, g, w1, w3, w2)


_E_HALVES = 2
_E_HALF = D_FF_EXPERT // _E_HALVES


def _moe_kernel(x_ref, g_ref, wr_ref, br_ref, w1_ref, w3_ref, w2_ref, o_ref,
                h_s, comb_s, acc_s):
    e = pl.program_id(1)
    f = pl.program_id(2)

    @pl.when((e == 0) & (f == 0))
    def _():
        x = x_ref[...]
        h = _rms(x, g_ref[...])
        h_s[...] = h.astype(BF16)
        acc_s[...] = x
        logits = jnp.dot(h, wr_ref[...], preferred_element_type=F32,
                         precision=lax.Precision.HIGHEST) + br_ref[...]
        lane = lax.broadcasted_iota(jnp.int32, logits.shape, 1)
        logits = jnp.where(lane < N_EXPERTS, logits, NEG_INF)
        m1 = jnp.max(logits, axis=-1, keepdims=True)
        i1 = jnp.min(jnp.where(logits == m1, lane, ROUTER_PAD), axis=-1, keepdims=True)
        rest = jnp.where(lane == i1, NEG_INF, logits)
        m2 = jnp.max(rest, axis=-1, keepdims=True)
        i2 = jnp.min(jnp.where(rest == m2, lane, ROUTER_PAD), axis=-1, keepdims=True)
        e2 = jnp.exp(m2 - m1)
        den = 1.0 + e2
        comb_s[...] = jnp.where(lane == i1, 1.0 / den, 0.0) + jnp.where(lane == i2, e2 / den, 0.0)

    h = h_s[...]
    u = jnp.dot(h, w1_ref[...], preferred_element_type=F32)
    g = jnp.dot(h, w3_ref[...], preferred_element_type=F32)
    mid = (jax.nn.silu(u) * g).astype(BF16)
    y = jnp.dot(mid, w2_ref[...], preferred_element_type=F32)
    lane = lax.broadcasted_iota(jnp.int32, comb_s.shape, 1)
    scale = jnp.sum(jnp.where(lane == e, comb_s[...], 0.0), axis=-1, keepdims=True)
    acc_s[...] += scale * y

    @pl.when((e == N_EXPERTS - 1) & (f == _E_HALVES - 1))
    def _():
        o_ref[...] = acc_s[...]


def _moe(x, g, wr, br, w1, w3, w2):
    t = x.shape[0]
    tm = TOKEN_TILE
    row = pl.BlockSpec((tm, D_MODEL), lambda i, e, f: (i, 0))
    return pl.pallas_call(
        _moe_kernel,
        grid=(t // tm, N_EXPERTS, _E_HALVES),
        in_specs=[row, _const_spec((1, D_MODEL)), _const_spec((D_MODEL, ROUTER_PAD)),
                  _const_spec((1, ROUTER_PAD)),
                  pl.BlockSpec((None, D_MODEL, _E_HALF), lambda i, e, f: (e, 0, f)),
                  pl.BlockSpec((None, D_MODEL, _E_HALF), lambda i, e, f: (e, 0, f)),
                  pl.BlockSpec((None, _E_HALF, D_MODEL), lambda i, e, f: (e, f, 0))],
        out_specs=row,
        out_shape=jax.ShapeDtypeStruct((t, D_MODEL), F32),
        scratch_shapes=[pltpu.VMEM((tm, D_MODEL), BF16), pltpu.VMEM((tm, ROUTER_PAD), F32),
                        pltpu.VMEM((tm, D_MODEL), F32)],
        compiler_params=_params("parallel", "arbitrary", "arbitrary"),
        name="moe",
    )(x, g, wr, br, w1, w3, w2)


def _norm_kernel(x_ref, g_ref, o_ref):
    o_ref[...] = _rms(x_ref[...], g_ref[...])


def _final_norm(x, g):
    t = x.shape[0]
    tm = TOKEN_TILE
    row = pl.BlockSpec((tm, D_MODEL), lambda i: (i, 0))
    return pl.pallas_call(
        _norm_kernel,
        grid=(t // tm,),
        in_specs=[row, _const_spec((1, D_MODEL))],
        out_specs=row,
        out_shape=jax.ShapeDtypeStruct((t, D_MODEL), F32),
        compiler_params=_params("parallel"),
        name="final_norm",
    )(x, g)


def _rel_bias(table, n_q, n_k, offset):
    dist = offset + jnp.arange(n_q)[:, None] - jnp.arange(n_k)[None, :]
    idx = jnp.clip(dist, -REL_CLIP, REL_CLIP) + REL_CLIP
    return jnp.take(table, idx, axis=1).astype(F32)


def _gmlp_mats(ws, bs, lc):
    pos = np.arange(lc)
    mask = (pos[None, :] // CHUNK) <= (pos[:, None] // CHUNK)
    w = jnp.where(mask[None], ws[:, :lc, :lc], 0)
    reps = A_CHUNK // lc
    if reps > 1:
        eye = jnp.eye(reps, dtype=w.dtype)
        w = jnp.einsum('rs,gij->grisj', eye, w).reshape(A_GROUPS, A_CHUNK, A_CHUNK)
    b = jnp.tile(bs[:, :lc], (1, reps))
    bsb = jnp.broadcast_to(b[:, :, None], (A_GROUPS, A_CHUNK, A_GROUP_DIM))
    return w.astype(BF16), bsb.astype(F32)


def kernel(x_prompt, x_sample, cache_attn_k, cache_attn_v, state_conv, norm_mix_g, w_in, gmlp_norm_g, gmlp_ws, gmlp_bs, attn_rel_bias, conv_w, w_branch, w_out, norm_ffn_g, ffn_w1, ffn_w3, ffn_w2, moe_router, moe_router_b, moe_w1, moe_w3, moe_w2, final_norm_g):
    nb, s, d = x_prompt.shape
    nbs, t, _ = x_sample.shape
    keep = cache_attn_k.shape[2]
    kv_keep = min(REACH, s)
    xp = x_prompt.reshape(nb * s, d)
    xs = x_sample.reshape(nbs * t, d)
    gf = final_norm_g.reshape(1, d)

    pk, pv, pc, sk, sv, sc, sg_out = [], [], [], [], [], [], []
    for l in range(DEPTH):
        g_mix = norm_mix_g[l].reshape(1, d)
        g_v = gmlp_norm_g[l].reshape(1, A_WIDTH)
        w_in_b = w_in[l].astype(BF16)
        wb_b = w_branch[l].astype(BF16)
        wo_b = w_out[l].astype(BF16)
        cw = conv_w[l]
        bias_p = _rel_bias(attn_rel_bias[l], CHUNK, REACH + CHUNK, REACH)
        bias_s = _rel_bias(attn_rel_bias[l], t, keep + t, keep)
        wm_p, bsb_p = _gmlp_mats(gmlp_ws[l], gmlp_bs[l], A_CHUNK)
        wm_s, bsb_s = _gmlp_mats(gmlp_ws[l], gmlp_bs[l], t)

        ug, vn, q, k, v, zc, gb, sgt = _inproj(xp, g_mix, w_in_b, g_v)
        bo, co = _seq_prompt(q, k, v, zc, gb, bias_p, cw, nb, s)
        xp = _merge(ug, vn, bo, co, sgt, xp, wm_p, bsb_p, wb_b, wo_b)
        pk.append(k.reshape(nb, s, HEADS, HEAD_DIM)[:, s - kv_keep:])
        pv.append(v.reshape(nb, s, HEADS, HEAD_DIM)[:, s - kv_keep:])
        pc.append(zc.reshape(nb, s, C_WIDTH)[:, s - (CONV_WIDTH - 1):])

        ug, vn, q, k, v, zc, gb, sgt = _inproj(xs, g_mix, w_in_b, g_v)
        ck = cache_attn_k[l].reshape(nbs, keep, B_WIDTH)
        cv = cache_attn_v[l].reshape(nbs, keep, B_WIDTH)
        bo, co, ns = _seq_sample(q, k, v, ck, cv, zc, gb, state_conv[l], bias_s, cw, nbs, t)
        xs = _merge(ug, vn, bo, co, sgt, xs, wm_s, bsb_s, wb_b, wo_b)
        sk.append(k.reshape(nbs, t, HEADS, HEAD_DIM))
        sv.append(v.reshape(nbs, t, HEADS, HEAD_DIM))
        sc.append(ns)
        sg_out.append(vn.reshape(nbs, t, A_WIDTH))

        g_ffn = norm_ffn_g[l].reshape(1, d)
        i = l // 2
        if l % 2 == 0:
            w1 = ffn_w1[i].astype(BF16)
            w3 = ffn_w3[i].astype(BF16)
            w2 = ffn_w2[i].astype(BF16)
            xp = _ffn(xp, g_ffn, w1, w3, w2)
            xs = _ffn(xs, g_ffn, w1, w3, w2)
        else:
            wr = jnp.pad(moe_router[i], ((0, 0), (0, ROUTER_PAD - N_EXPERTS)))
            br = jnp.pad(moe_router_b[i], (0, ROUTER_PAD - N_EXPERTS)).reshape(1, ROUTER_PAD)
            w1 = moe_w1[i].astype(BF16)
            w3 = moe_w3[i].astype(BF16)
            w2 = moe_w2[i].astype(BF16)
            xp = _moe(xp, g_ffn, wr, br, w1, w3, w2)
            xs = _moe(xs, g_ffn, wr, br, w1, w3, w2)

    y_prompt = _final_norm(xp, gf).reshape(nb, s, d)
    y_sample = _final_norm(xs, gf).reshape(nbs, t, d)
    return (y_prompt, y_sample, jnp.stack(pk), jnp.stack(pv), jnp.stack(pc),
            jnp.stack(sk), jnp.stack(sv), jnp.stack(sc), jnp.stack(sg_out))
```

```python
import functools

import jax
import jax.numpy as jnp
import numpy as np
from jax import lax
from jax.experimental import pallas as pl
from jax.experimental.pallas import tpu as pltpu

F32 = jnp.float32
BF16 = jnp.bfloat16

D_MODEL = 1024
DEPTH = 2
CHUNK = 64
A_CHUNK = 128
A_WIDTH = 512
A_GROUPS = 4
A_GROUP_DIM = 128
HEAD_DIM = 64
HEADS = 8
B_WIDTH = 512
REACH = 512
REL_CLIP = 128
C_WIDTH = 512
CONV_WIDTH = 3
N_BRANCH = 3
D_FF = 2816
N_EXPERTS = 8
TOP_K = 2
D_FF_EXPERT = 3584
EPS = 1e-6
NEG_INF = -1e30

_O_UA, _O_VA, _O_Q, _O_K, _O_V, _O_CIN, _O_GB, _O_GC, _O_GATES = (
    0, 512, 1024, 1536, 2048, 2560, 3072, 3584, 4096)
IN_COLS = 7168

TOKEN_TILE = 512
EXPERT_TILE = 512
ATT_TQ = 256
VMEM_LIMIT = 56 * 1024 * 1024
LANES = 128
SUBLANES = 8


def _params(*sem):
    return pltpu.CompilerParams(dimension_semantics=sem, vmem_limit_bytes=VMEM_LIMIT)


def _const_spec(shape):
    nd = len(shape)
    return pl.BlockSpec(shape, lambda *_: (0,) * nd, pipeline_mode=pl.Buffered(1))


def _rms(x, g):
    ms = jnp.mean(x * x, axis=-1, keepdims=True)
    return x * lax.rsqrt(ms + EPS) * g


def _inproj_kernel(x_ref, g_ref, w_ref, gv_ref,
                   ug_ref, vn_ref, q_ref, k_ref, v_ref, zc_ref, gb_ref, sg_ref):
    h = _rms(x_ref[...], g_ref[...]).astype(BF16)

    def col(a, n=512):
        return jnp.dot(h, w_ref[:, a:a + n], preferred_element_type=F32)

    ug_ref[...] = jax.nn.gelu(col(_O_UA)).astype(BF16)
    vn_ref[...] = _rms(jax.nn.gelu(col(_O_VA)), gv_ref[...])
    q_ref[...] = (col(_O_Q) * (HEAD_DIM ** -0.5)).astype(BF16)
    k_ref[...] = col(_O_K)
    v_ref[...] = col(_O_V)
    zc_ref[...] = col(_O_GC) * col(_O_CIN)
    gb_ref[...] = col(_O_GB).astype(BF16)
    for j in range(N_BRANCH * D_MODEL // 512):
        sg_ref[:, j * 512:(j + 1) * 512] = jax.nn.sigmoid(col(_O_GATES + j * 512)).astype(BF16)


def _inproj(x, g, w_in, g_v):
    t = x.shape[0]
    tm = TOKEN_TILE
    row = lambda n: pl.BlockSpec((tm, n), lambda i: (i, 0))
    outs = [
        jax.ShapeDtypeStruct((t, A_WIDTH), BF16),
        jax.ShapeDtypeStruct((t, A_WIDTH), F32),
        jax.ShapeDtypeStruct((t, B_WIDTH), BF16),
        jax.ShapeDtypeStruct((t, B_WIDTH), F32),
        jax.ShapeDtypeStruct((t, B_WIDTH), F32),
        jax.ShapeDtypeStruct((t, C_WIDTH), F32),
        jax.ShapeDtypeStruct((t, C_WIDTH), BF16),
        jax.ShapeDtypeStruct((t, N_BRANCH * D_MODEL), BF16),
    ]
    return pl.pallas_call(
        _inproj_kernel,
        grid=(t // tm,),
        in_specs=[row(D_MODEL), _const_spec((1, D_MODEL)), _const_spec((D_MODEL, IN_COLS)),
                  _const_spec((1, A_WIDTH))],
        out_specs=[row(A_WIDTH), row(A_WIDTH), row(B_WIDTH), row(B_WIDTH), row(B_WIDTH),
                   row(C_WIDTH), row(C_WIDTH), row(N_BRANCH * D_MODEL)],
        out_shape=outs,
        compiler_params=_params("parallel"),
        name="inproj",
    )(x, g, w_in, g_v)


def _attend_tile(q_ref, kb, vt, bias_ref, bo_ref, q0, kblk0, nblk, bias_r0):
    tq = ATT_TQ
    nk = nblk * tq
    lane = lax.broadcasted_iota(jnp.int32, (tq, LANES), 1)
    orow = lax.broadcasted_iota(jnp.int32, (LANES, tq), 0)
    k0 = kblk0 * tq
    if not isinstance(k0, int):
        k0 = pl.multiple_of(k0, tq)
    for p in range(HEADS // 2):
        cs = slice(p * LANES, (p + 1) * LANES)
        qp = q_ref[pl.ds(q0, tq), cs]
        kp = kb[pl.ds(k0, nk), cs]
        outs = []
        for hh in range(2):
            sel = (lane < HEAD_DIM) if hh == 0 else (lane >= HEAD_DIM)
            qh = jnp.where(sel, qp, jnp.zeros_like(qp))
            st = lax.dot_general(kp, qh, (((1,), (1,)), ((), ())), preferred_element_type=F32)
            st = st + bias_ref[2 * p + hh, bias_r0:bias_r0 + nk, :]
            m = jnp.max(st, axis=0, keepdims=True)
            e = jnp.exp(st - m)
            l = jnp.sum(e, axis=0, keepdims=True)
            eb = e.astype(BF16)
            o = None
            for b in range(nblk):
                ob = jnp.dot(vt[kblk0 + b, cs, :], eb[b * tq:(b + 1) * tq, :],
                             preferred_element_type=F32)
                o = ob if o is None else o + ob
            outs.append(o / l)
        ot = jnp.where(orow < HEAD_DIM, outs[0], outs[1])
        bo_ref[pl.ds(q0, tq), cs] = ot.T.astype(BF16)


def _attn_prompt_kernel(q_ref, k_ref, v_ref, bias_ref, bo_ref, kb, vt):
    s = q_ref.shape[0]
    tq = ATT_TQ
    nwin = REACH // tq + 1
    kb[...] = k_ref[...].astype(BF16)
    for jb in range(s // tq):
        vt[jb] = v_ref[jb * tq:(jb + 1) * tq, :].T.astype(BF16)
    for t in range(nwin - 1):
        _attend_tile(q_ref, kb, vt, bias_ref, bo_ref, t * tq, 0, t + 1, (nwin - 1 - t) * tq)

    def tile(t, carry):
        _attend_tile(q_ref, kb, vt, bias_ref, bo_ref, pl.multiple_of(t * tq, tq),
                     t - (nwin - 1), nwin, 0)
        return carry

    lax.fori_loop(nwin - 1, s // tq, tile, 0)


def _attn_prompt(q, k, v, bias, nb, s):
    seq = lambda n: pl.BlockSpec((s, n), lambda b: (b, 0))
    tq = ATT_TQ
    return pl.pallas_call(
        _attn_prompt_kernel,
        grid=(nb,),
        in_specs=[seq(B_WIDTH), seq(B_WIDTH), seq(B_WIDTH),
                  _const_spec((HEADS, REACH + tq, tq))],
        out_specs=seq(B_WIDTH),
        out_shape=jax.ShapeDtypeStruct((nb * s, B_WIDTH), BF16),
        scratch_shapes=[pltpu.VMEM((s, B_WIDTH), BF16),
                        pltpu.VMEM((s // tq, B_WIDTH, tq), BF16)],
        compiler_params=_params("parallel"),
        name="attn_prompt",
    )(q, k, v, bias)


_ZPAD = 8


def _conv_rows(zext_ref, gb, cw_ref, n):
    z0 = zext_ref[pl.ds(_ZPAD, n), :]
    z1 = zext_ref[pl.ds(_ZPAD - 1, n), :]
    z2 = zext_ref[pl.ds(_ZPAD - 2, n), :]
    y = z2 * cw_ref[0:1, :] + z1 * cw_ref[1:2, :] + z0 * cw_ref[2:3, :]
    return (gb.astype(F32) * y).astype(BF16)


def _seq_sample_kernel(q_ref, k_ref, v_ref, ck_ref, cv_ref, zc_ref, gb_ref, cs_ref, bias_ref, cw_ref,
                       bo_ref, co_ref, ns_ref, kk, vv, zext):
    t = q_ref.shape[0]
    keep = ck_ref.shape[0]
    kk[0:keep, :] = ck_ref[...].astype(BF16)
    vv[0:keep, :] = cv_ref[...].astype(BF16)
    kk[keep:, :] = k_ref[...].astype(BF16)
    vv[keep:, :] = v_ref[...].astype(BF16)
    hist = CONV_WIDTH - 1
    zext[_ZPAD - hist:_ZPAD, :] = cs_ref[...]
    zext[_ZPAD:, :] = zc_ref[...]

    lane = lax.broadcasted_iota(jnp.int32, (t, LANES), 1)
    for p in range(HEADS // 2):
        cs = slice(p * LANES, (p + 1) * LANES)
        qp = q_ref[:, cs]
        kp = kk[:, cs]
        vp = vv[:, cs]
        outs = []
        for hh in range(2):
            sel = (lane < HEAD_DIM) if hh == 0 else (lane >= HEAD_DIM)
            qh = jnp.where(sel, qp, jnp.zeros_like(qp))
            s = lax.dot_general(qh, kp, (((1,), (1,)), ((), ())), preferred_element_type=F32)
            s = s + bias_ref[2 * p + hh]
            m = jnp.max(s, axis=-1, keepdims=True)
            e = jnp.exp(s - m)
            l = jnp.sum(e, axis=-1, keepdims=True)
            outs.append(jnp.dot(e.astype(BF16), vp, preferred_element_type=F32) / l)
        bo_ref[:, cs] = jnp.where(lane < HEAD_DIM, outs[0], outs[1]).astype(BF16)

    co_ref[...] = _conv_rows(zext, gb_ref[...], cw_ref, t)
    ns_ref[...] = zext[_ZPAD + t - hist:_ZPAD + t, :]


def _seq_sample(q, k, v, ck, cv, zc, gb, cs, bias, cw, nb, t):
    keep = ck.shape[1]
    hist = CONV_WIDTH - 1
    seq = lambda n: pl.BlockSpec((t, n), lambda b: (b, 0))
    per_b = lambda r, n: pl.BlockSpec((None, r, n), lambda b: (b, 0, 0))
    return pl.pallas_call(
        _seq_sample_kernel,
        grid=(nb,),
        in_specs=[seq(B_WIDTH), seq(B_WIDTH), seq(B_WIDTH), per_b(keep, B_WIDTH), per_b(keep, B_WIDTH),
                  seq(C_WIDTH), seq(C_WIDTH), per_b(hist, C_WIDTH),
                  _const_spec((HEADS, t, keep + t)), _const_spec((CONV_WIDTH, C_WIDTH))],
        out_specs=[seq(B_WIDTH), seq(C_WIDTH), per_b(hist, C_WIDTH)],
        out_shape=[jax.ShapeDtypeStruct((nb * t, B_WIDTH), BF16),
                   jax.ShapeDtypeStruct((nb * t, C_WIDTH), BF16),
                   jax.ShapeDtypeStruct((nb, hist, C_WIDTH), F32)],
        scratch_shapes=[pltpu.VMEM((keep + t, B_WIDTH), BF16),
                        pltpu.VMEM((keep + t, B_WIDTH), BF16),
                        pltpu.VMEM((_ZPAD + t, C_WIDTH), F32)],
        compiler_params=_params("parallel"),
        name="seq_sample",
    )(q, k, v, ck, cv, zc, gb, cs, bias, cw)


def _merge_body(ug_ref, vn_ref, bo_ref, co, sg_ref, x_ref, wm_ref, bsb_ref, wb_ref, wo_ref, o_ref, a_s):
    tm = ug_ref.shape[0]
    for n in range(tm // A_CHUNK):
        rows = slice(n * A_CHUNK, (n + 1) * A_CHUNK)
        for g in range(A_GROUPS):
            cols = slice(g * A_GROUP_DIM, (g + 1) * A_GROUP_DIM)
            mixed = jnp.dot(wm_ref[g], vn_ref[rows, cols].astype(BF16),
                            preferred_element_type=F32) + bsb_ref[g]
            a_s[rows, cols] = (ug_ref[rows, cols].astype(F32) * mixed).astype(BF16)
    m = None
    for n, br in enumerate((a_s[...], bo_ref[...], co)):
        p = jnp.dot(br, wb_ref[n], preferred_element_type=F32)
        term = sg_ref[:, n * D_MODEL:(n + 1) * D_MODEL].astype(F32) * p
        m = term if m is None else m + term
    y = jnp.dot(m.astype(BF16), wo_ref[...], preferred_element_type=F32)
    o_ref[...] = x_ref[...] + y


def _merge_conv_kernel(tiles_per_seq, ug_ref, vn_ref, bo_ref, zc_ref, halo_ref, gb_ref, sg_ref, x_ref,
                       wm_ref, bsb_ref, cw_ref, wb_ref, wo_ref, o_ref, a_s, zext):
    tm = ug_ref.shape[0]
    first = pl.program_id(0) % tiles_per_seq == 0
    zext[0:_ZPAD, :] = jnp.where(first, 0.0, halo_ref[...])
    zext[_ZPAD:, :] = zc_ref[...]
    co = _conv_rows(zext, gb_ref[...], cw_ref, tm)
    _merge_body(ug_ref, vn_ref, bo_ref, co, sg_ref, x_ref, wm_ref, bsb_ref, wb_ref, wo_ref, o_ref, a_s)


def _merge_kernel(ug_ref, vn_ref, bo_ref, co_ref, sg_ref, x_ref, wm_ref, bsb_ref, wb_ref, wo_ref,
                  o_ref, a_s):
    _merge_body(ug_ref, vn_ref, bo_ref, co_ref[...], sg_ref, x_ref, wm_ref, bsb_ref, wb_ref, wo_ref,
                o_ref, a_s)


def _merge(ug, vn, bo, sg, x, wm, bsb, wb, wo, *, co=None, zc=None, gb=None, cw=None, seq_len=None):
    t = x.shape[0]
    tm = TOKEN_TILE
    row = lambda n: pl.BlockSpec((tm, n), lambda i: (i, 0))
    weights = [_const_spec((A_GROUPS, A_CHUNK, A_CHUNK)), _const_spec((A_GROUPS, A_CHUNK, A_GROUP_DIM))]
    tail = [_const_spec((N_BRANCH, A_WIDTH, D_MODEL)), _const_spec((D_MODEL, D_MODEL))]
    scratch = [pltpu.VMEM((tm, A_WIDTH), BF16)]
    if co is not None:
        body = _merge_kernel
        in_specs = [row(A_WIDTH), row(A_WIDTH), row(B_WIDTH), row(C_WIDTH), row(N_BRANCH * D_MODEL),
                    row(D_MODEL)] + weights + tail
        args = (ug, vn, bo, co, sg, x, wm, bsb, wb, wo)
    else:
        assert seq_len % tm == 0
        body = functools.partial(_merge_conv_kernel, seq_len // tm)
        per = tm // _ZPAD
        halo = pl.BlockSpec((_ZPAD, C_WIDTH), lambda i: (jnp.maximum(i * per - 1, 0), 0))
        in_specs = [row(A_WIDTH), row(A_WIDTH), row(B_WIDTH), row(C_WIDTH), halo, row(C_WIDTH),
                    row(N_BRANCH * D_MODEL), row(D_MODEL)] + weights + [
                        _const_spec((CONV_WIDTH, C_WIDTH))] + tail
        args = (ug, vn, bo, zc, zc, gb, sg, x, wm, bsb, cw, wb, wo)
        scratch = scratch + [pltpu.VMEM((_ZPAD + tm, C_WIDTH), F32)]
    return pl.pallas_call(
        body,
        grid=(t // tm,),
        in_specs=in_specs,
        out_specs=row(D_MODEL),
        out_shape=jax.ShapeDtypeStruct((t, D_MODEL), F32),
        scratch_shapes=scratch,
        compiler_params=_params("parallel"),
        name="merge",
    )(*args)


_FF_SPLITS = ((0, 1280), (1280, 1536))


def _ffn_kernel(x_ref, g_ref, w1_ref, w3_ref, w2_ref, o_ref):
    x = x_ref[...]
    h = _rms(x, g_ref[...]).astype(BF16)
    y = x
    for a, n in _FF_SPLITS:
        u = jnp.dot(h, w1_ref[:, a:a + n], preferred_element_type=F32)
        g = jnp.dot(h, w3_ref[:, a:a + n], preferred_element_type=F32)
        mid = (jax.nn.silu(u) * g).astype(BF16)
        y = y + jnp.dot(mid, w2_ref[a:a + n, :], preferred_element_type=F32)
    o_ref[...] = y


def _ffn(x, g, w1, w3, w2):
    t = x.shape[0]
    tm = TOKEN_TILE
    row = pl.BlockSpec((tm, D_MODEL), lambda i: (i, 0))
    return pl.pallas_call(
        _ffn_kernel,
        grid=(t // tm,),
        in_specs=[row, _const_spec((1, D_MODEL)), _const_spec((D_MODEL, D_FF)),
                  _const_spec((D_MODEL, D_FF)), _const_spec((D_FF, D_MODEL))],
        out_specs=row,
        out_shape=jax.ShapeDtypeStruct((t, D_MODEL), F32),
        compiler_params=_params("parallel"),
        name="ffn",
    )(x, g, w1, w3, w2)


_M_E1, _M_E2, _M_R1, _M_R2, _M_W1, _M_W2 = range(6)


def _router_kernel(x_ref, g_ref, wrt_ref, br_ref, tri_ref, meta_ref, cnt_ref, cnt_s):
    @pl.when(pl.program_id(0) == 0)
    def _():
        cnt_s[...] = jnp.zeros_like(cnt_s)

    h = _rms(x_ref[...], g_ref[...])
    logits = lax.dot_general(wrt_ref[...], h, (((1,), (1,)), ((), ())), preferred_element_type=F32,
                             precision=lax.Precision.HIGHEST) + br_ref[...]
    eid = lax.broadcasted_iota(jnp.int32, logits.shape, 0)
    m1 = jnp.max(logits, axis=0, keepdims=True)
    i1 = jnp.min(jnp.where(logits == m1, eid, N_EXPERTS), axis=0, keepdims=True)
    rest = jnp.where(eid == i1, NEG_INF, logits)
    m2 = jnp.max(rest, axis=0, keepdims=True)
    i2 = jnp.min(jnp.where(rest == m2, eid, N_EXPERTS), axis=0, keepdims=True)
    e2 = jnp.exp(m2 - m1)
    den = 1.0 + e2
    oh1 = (eid == i1).astype(F32)
    oh2 = (eid == i2).astype(F32)
    both = oh1 + oh2
    before = jnp.dot(both.astype(BF16), tri_ref[...], preferred_element_type=F32) + cnt_s[:, 0:1]
    meta_ref[_M_E1:_M_E1 + 1, :] = i1.astype(F32)
    meta_ref[_M_E2:_M_E2 + 1, :] = i2.astype(F32)
    meta_ref[_M_R1:_M_R1 + 1, :] = jnp.sum(oh1 * before, axis=0, keepdims=True)
    meta_ref[_M_R2:_M_R2 + 1, :] = jnp.sum(oh2 * before, axis=0, keepdims=True)
    meta_ref[_M_W1:_M_W1 + 1, :] = 1.0 / den
    meta_ref[_M_W2:_M_W2 + 1, :] = e2 / den
    meta_ref[_M_W2 + 1:, :] = jnp.zeros((SUBLANES - _M_W2 - 1, logits.shape[1]), F32)
    cnt_s[...] = cnt_s[...] + jnp.sum(both, axis=1, keepdims=True)
    cnt_ref[...] = cnt_s[...]


def _router(x, g, wrt, br):
    t = x.shape[0]
    tm = TOKEN_TILE
    nt = t // tm
    tri = jnp.asarray(np.triu(np.ones((tm, tm), np.float32), 1), BF16)
    return pl.pallas_call(
        _router_kernel,
        grid=(nt,),
        in_specs=[pl.BlockSpec((tm, D_MODEL), lambda i: (i, 0)), _const_spec((1, D_MODEL)),
                  _const_spec((N_EXPERTS, D_MODEL)), _const_spec((N_EXPERTS, 1)), _const_spec((tm, tm))],
        out_specs=[pl.BlockSpec((None, SUBLANES, tm), lambda i: (i, 0, 0)),
                   pl.BlockSpec((N_EXPERTS, LANES), lambda i: (0, 0))],
        out_shape=[jax.ShapeDtypeStruct((nt, SUBLANES, tm), F32),
                   jax.ShapeDtypeStruct((N_EXPERTS, LANES), F32)],
        scratch_shapes=[pltpu.VMEM((N_EXPERTS, LANES), F32)],
        compiler_params=_params("arbitrary"),
        name="router",
    )(x, g, wrt, br, tri)


def _row_copy(src, s, dst, d, sem):
    return pltpu.make_async_copy(src.at[pl.ds(s, 1)], dst.at[pl.ds(d, 1)], sem)


def _dispatch_kernel(slots_ref, x_hbm, xs_in, xs_hbm, sem):
    del xs_in
    tm = TOKEN_TILE
    base = pl.program_id(0) * tm

    def issue(r, c):
        for k in range(TOP_K):
            _row_copy(x_hbm, base + r, xs_hbm, slots_ref[0, k * tm + r], sem).start()
        return c

    lax.fori_loop(0, tm, issue, 0, unroll=8)

    def drain(r, c):
        for k in range(TOP_K):
            _row_copy(x_hbm, base + r, xs_hbm, slots_ref[0, k * tm + r], sem).wait()
        return c

    lax.fori_loop(0, tm, drain, 0, unroll=8)


def _dispatch(slots, x, n_rows):
    t = x.shape[0]
    tm = TOKEN_TILE
    xs0 = jnp.zeros((n_rows, D_MODEL), F32)
    return pl.pallas_call(
        _dispatch_kernel,
        grid=(t // tm,),
        in_specs=[pl.BlockSpec((None, 1, TOP_K * tm), lambda i: (i, 0, 0), memory_space=pltpu.SMEM),
                  pl.BlockSpec(memory_space=pl.ANY), pl.BlockSpec(memory_space=pl.ANY)],
        out_specs=pl.BlockSpec(memory_space=pl.ANY),
        out_shape=jax.ShapeDtypeStruct((n_rows, D_MODEL), F32),
        scratch_shapes=[pltpu.SemaphoreType.DMA(())],
        input_output_aliases={2: 0},
        compiler_params=_params("arbitrary"),
        name="dispatch",
    )(slots, x, xs0)


_E_COLS = 512


def _expert_kernel(te_ref, nu_ref, xs_ref, g_ref, w1_ref, w3_ref, w2_ref, ys_ref):
    i = pl.program_id(0)

    @pl.when(i < nu_ref[0])
    def _():
        h = _rms(xs_ref[...], g_ref[...]).astype(BF16)
        y = None
        for c in range(D_FF_EXPERT // _E_COLS):
            cols = slice(c * _E_COLS, (c + 1) * _E_COLS)
            u = jnp.dot(h, w1_ref[:, cols], preferred_element_type=F32)
            g = jnp.dot(h, w3_ref[:, cols], preferred_element_type=F32)
            mid = (jax.nn.silu(u) * g).astype(BF16)
            part = jnp.dot(mid, w2_ref[cols, :], preferred_element_type=F32)
            y = part if y is None else y + part
        ys_ref[...] = y

    @pl.when(i >= nu_ref[0])
    def _():
        ys_ref[...] = jnp.zeros_like(ys_ref)


def _experts(tile_expert, n_used, xs, g, w1, w3, w2):
    n_rows = xs.shape[0]
    tm = EXPERT_TILE
    row = pl.BlockSpec((tm, D_MODEL), lambda i, te, nu: (i, 0))
    wspec = lambda r, c: pl.BlockSpec((None, r, c), lambda i, te, nu: (te[i], 0, 0),
                                      pipeline_mode=pl.Buffered(1))
    return pl.pallas_call(
        _expert_kernel,
        grid_spec=pltpu.PrefetchScalarGridSpec(
            num_scalar_prefetch=2,
            grid=(n_rows // tm,),
            in_specs=[row, pl.BlockSpec((1, D_MODEL), lambda i, te, nu: (0, 0),
                                        pipeline_mode=pl.Buffered(1)),
                      wspec(D_MODEL, D_FF_EXPERT), wspec(D_MODEL, D_FF_EXPERT),
                      wspec(D_FF_EXPERT, D_MODEL)],
            out_specs=row,
        ),
        out_shape=jax.ShapeDtypeStruct((n_rows, D_MODEL), F32),
        compiler_params=_params("arbitrary"),
        name="experts",
    )(tile_expert, n_used, xs, g, w1, w3, w2)


def _combine_kernel(slots_ref, meta_ref, x_ref, ys_hbm, gf_ref, o_ref, rows, sem):
    tm = TOKEN_TILE

    def issue(r, c):
        for k in range(TOP_K):
            _row_copy(ys_hbm, slots_ref[0, k * tm + r], rows.at[k], r, sem).start()
        return c

    lax.fori_loop(0, tm, issue, 0, unroll=8)

    def drain(r, c):
        for k in range(TOP_K):
            _row_copy(ys_hbm, slots_ref[0, k * tm + r], rows.at[k], r, sem).wait()
        return c

    lax.fori_loop(0, tm, drain, 0, unroll=8)

    meta = jnp.concatenate([meta_ref[...], jnp.zeros((LANES - SUBLANES, tm), F32)], axis=0)
    mt = meta.T
    w1 = mt[:, _M_W1:_M_W1 + 1]
    w2 = mt[:, _M_W2:_M_W2 + 1]
    x = x_ref[...] + w1 * rows[0] + w2 * rows[1]
    o_ref[...] = _rms(x, gf_ref[...])


def _combine(slots, meta, x, ys, gf):
    t = x.shape[0]
    tm = TOKEN_TILE
    row = pl.BlockSpec((tm, D_MODEL), lambda i: (i, 0))
    return pl.pallas_call(
        _combine_kernel,
        grid=(t // tm,),
        in_specs=[pl.BlockSpec((None, 1, TOP_K * tm), lambda i: (i, 0, 0), memory_space=pltpu.SMEM),
                  pl.BlockSpec((None, SUBLANES, tm), lambda i: (i, 0, 0)),
                  row, pl.BlockSpec(memory_space=pl.ANY), _const_spec((1, D_MODEL))],
        out_specs=row,
        out_shape=jax.ShapeDtypeStruct((t, D_MODEL), F32),
        scratch_shapes=[pltpu.VMEM((TOP_K, tm, D_MODEL), F32), pltpu.SemaphoreType.DMA(())],
        compiler_params=_params("arbitrary"),
        name="combine",
    )(slots, meta, x, ys, gf)


def _moe_final(x, g, wr, br, w1, w3, w2, gf):
    t = x.shape[0]
    tm = EXPERT_TILE
    n_tiles = (TOP_K * t + N_EXPERTS * (tm - 1) + tm - 1) // tm
    meta, cnt = _router(x, g, wr.T, br.reshape(N_EXPERTS, 1))

    counts = cnt[:, 0].astype(jnp.int32)
    tiles = (counts + tm - 1) // tm
    tile_end = jnp.cumsum(tiles)
    row_off = (tile_end - tiles) * tm
    n_used = tile_end[-1]
    tid = jnp.minimum(jnp.arange(n_tiles, dtype=jnp.int32), n_used - 1)
    tile_expert = jnp.sum(tid[:, None] >= tile_end[None, :], axis=1).astype(jnp.int32)

    def slot(e_row, r_row):
        e = meta[:, e_row, :].astype(jnp.int32)
        off = jnp.sum(jnp.where(e[..., None] == jnp.arange(N_EXPERTS), row_off, 0), axis=-1)
        return off + meta[:, r_row, :].astype(jnp.int32)

    slots = jnp.concatenate([slot(_M_E1, _M_R1), slot(_M_E2, _M_R2)], axis=1)[:, None, :]

    xs = _dispatch(slots, x, n_tiles * tm)
    ys = _experts(tile_expert, n_used.reshape(1), xs, g, w1, w3, w2)
    return _combine(slots, meta, x, ys, gf)


def _rel_bias(table, n_q, n_k, offset):
    dist = offset + jnp.arange(n_q)[:, None] - jnp.arange(n_k)[None, :]
    idx = jnp.clip(dist, -REL_CLIP, REL_CLIP) + REL_CLIP
    return jnp.take(table, idx, axis=1).astype(F32)


def _prompt_bias(table):
    tq = ATT_TQ
    bias = _rel_bias(table, tq, REACH + tq, REACH)
    qc = np.arange(tq)[:, None] // CHUNK
    kc = np.arange(REACH + tq)[None, :] // CHUNK
    band = (kc >= qc) & (kc <= qc + REACH // CHUNK)
    return jnp.swapaxes(jnp.where(band[None], bias, NEG_INF), 1, 2)


def _gmlp_mats(ws, bs, lc):
    pos = np.arange(lc)
    mask = (pos[None, :] // CHUNK) <= (pos[:, None] // CHUNK)
    w = jnp.where(mask[None], ws[:, :lc, :lc], 0)
    reps = A_CHUNK // lc
    if reps > 1:
        eye = jnp.eye(reps, dtype=w.dtype)
        w = jnp.einsum('rs,gij->grisj', eye, w).reshape(A_GROUPS, A_CHUNK, A_CHUNK)
    b = jnp.tile(bs[:, :lc], (1, reps))
    bsb = jnp.broadcast_to(b[:, :, None], (A_GROUPS, A_CHUNK, A_GROUP_DIM))
    return w.astype(BF16), bsb.astype(F32)


def kernel(x_prompt, x_sample, cache_attn_k, cache_attn_v, state_conv, norm_mix_g, w_in, gmlp_norm_g, gmlp_ws, gmlp_bs, attn_rel_bias, conv_w, w_branch, w_out, norm_ffn_g, ffn_w1, ffn_w3, ffn_w2, moe_router, moe_router_b, moe_w1, moe_w3, moe_w2, final_norm_g):
    nb, s, d = x_prompt.shape
    nbs, t, _ = x_sample.shape
    keep = cache_attn_k.shape[2]
    kv_keep = min(REACH, s)
    xp = x_prompt.reshape(nb * s, d)
    xs = x_sample.reshape(nbs * t, d)
    gf = final_norm_g.reshape(1, d)

    pk, pv, pc, sk, sv, sc, sg_out = [], [], [], [], [], [], []
    for l in range(DEPTH):
        g_mix = norm_mix_g[l].reshape(1, d)
        g_v = gmlp_norm_g[l].reshape(1, A_WIDTH)
        w_in_b = w_in[l].astype(BF16)
        wb_b = w_branch[l].astype(BF16)
        wo_b = w_out[l].astype(BF16)
        cw = conv_w[l]
        bias_p = _prompt_bias(attn_rel_bias[l])
        bias_s = _rel_bias(attn_rel_bias[l], t, keep + t, keep)
        wm_p, bsb_p = _gmlp_mats(gmlp_ws[l], gmlp_bs[l], A_CHUNK)
        wm_s, bsb_s = _gmlp_mats(gmlp_ws[l], gmlp_bs[l], t)

        ug, vn, q, k, v, zc, gb, sgt = _inproj(xp, g_mix, w_in_b, g_v)
        bo = _attn_prompt(q, k, v, bias_p, nb, s)
        xp = _merge(ug, vn, bo, sgt, xp, wm_p, bsb_p, wb_b, wo_b, zc=zc, gb=gb, cw=cw, seq_len=s)
        pk.append(k.reshape(nb, s, HEADS, HEAD_DIM)[:, s - kv_keep:])
        pv.append(v.reshape(nb, s, HEADS, HEAD_DIM)[:, s - kv_keep:])
        pc.append(zc.reshape(nb, s, C_WIDTH)[:, s - (CONV_WIDTH - 1):])

        ug, vn, q, k, v, zc, gb, sgt = _inproj(xs, g_mix, w_in_b, g_v)
        ck = cache_attn_k[l].reshape(nbs, keep, B_WIDTH)
        cv = cache_attn_v[l].reshape(nbs, keep, B_WIDTH)
        bo, co, ns = _seq_sample(q, k, v, ck, cv, zc, gb, state_conv[l], bias_s, cw, nbs, t)
        xs = _merge(ug, vn, bo, sgt, xs, wm_s, bsb_s, wb_b, wo_b, co=co)
        sk.append(k.reshape(nbs, t, HEADS, HEAD_DIM))
        sv.append(v.reshape(nbs, t, HEADS, HEAD_DIM))
        sc.append(ns)
        sg_out.append(vn.reshape(nbs, t, A_WIDTH))

        g_ffn = norm_ffn_g[l].reshape(1, d)
        i = l // 2
        if l % 2 == 0:
            w1 = ffn_w1[i].astype(BF16)
            w3 = ffn_w3[i].astype(BF16)
            w2 = ffn_w2[i].astype(BF16)
            xp = _ffn(xp, g_ffn, w1, w3, w2)
            xs = _ffn(xs, g_ffn, w1, w3, w2)
        else:
            assert l == DEPTH - 1
            y = _moe_final(jnp.concatenate([xp, xs], axis=0), g_ffn, moe_router[i], moe_router_b[i],
                           moe_w1[i].astype(BF16), moe_w3[i].astype(BF16), moe_w2[i].astype(BF16), gf)
            xp, xs = y[:nb * s], y[nb * s:]

    y_prompt = xp.reshape(nb, s, d)
    y_sample = xs.reshape(nbs, t, d)
    return (y_prompt, y_sample, jnp.stack(pk), jnp.stack(pv), jnp.stack(pc),
            jnp.stack(sk), jnp.stack(sv), jnp.stack(sc), jnp.stack(sg_out))
```

```python
import functools

import jax
import jax.numpy as jnp
import numpy as np
from jax import lax
from jax.experimental import pallas as pl
from jax.experimental.pallas import tpu as pltpu

F32 = jnp.float32
BF16 = jnp.bfloat16

D_MODEL = 1024
DEPTH = 2
CHUNK = 64
A_CHUNK = 128
A_WIDTH = 512
A_GROUPS = 4
A_GROUP_DIM = 128
HEAD_DIM = 64
HEADS = 8
B_WIDTH = 512
REACH = 512
REL_CLIP = 128
C_WIDTH = 512
CONV_WIDTH = 3
N_BRANCH = 3
D_FF = 2816
N_EXPERTS = 8
TOP_K = 2
D_FF_EXPERT = 3584
EPS = 1e-6
NEG_INF = -1e30

_O_UA, _O_VA, _O_Q, _O_K, _O_V, _O_CIN, _O_GB, _O_GC, _O_GATES = (
    0, 512, 1024, 1536, 2048, 2560, 3072, 3584, 4096)
IN_COLS = 7168

TOKEN_TILE = 512
EXPERT_TILE = 512
ATT_TQ = 256
VMEM_LIMIT = 56 * 1024 * 1024
LANES = 128
SUBLANES = 8


def _params(*sem):
    return pltpu.CompilerParams(dimension_semantics=sem, vmem_limit_bytes=VMEM_LIMIT)


def _const_spec(shape):
    nd = len(shape)
    return pl.BlockSpec(shape, lambda *_: (0,) * nd, pipeline_mode=pl.Buffered(1))


def _rms(x, g):
    ms = jnp.mean(x * x, axis=-1, keepdims=True)
    return x * lax.rsqrt(ms + EPS) * g


def _inproj_kernel(x_ref, g_ref, w_ref, gv_ref,
                   ug_ref, vn_ref, q_ref, k_ref, v_ref, zc_ref, gb_ref, sg_ref):
    h = _rms(x_ref[...], g_ref[...]).astype(BF16)

    def col(a, n=512):
        return jnp.dot(h, w_ref[:, a:a + n], preferred_element_type=F32)

    ug_ref[...] = jax.nn.gelu(col(_O_UA)).astype(BF16)
    vn_ref[...] = _rms(jax.nn.gelu(col(_O_VA)), gv_ref[...])
    q_ref[...] = (col(_O_Q) * (HEAD_DIM ** -0.5)).astype(BF16)
    k_ref[...] = col(_O_K)
    v_ref[...] = col(_O_V)
    zc_ref[...] = col(_O_GC) * col(_O_CIN)
    gb_ref[...] = col(_O_GB).astype(BF16)
    for j in range(N_BRANCH * D_MODEL // 512):
        sg_ref[:, j * 512:(j + 1) * 512] = jax.nn.sigmoid(col(_O_GATES + j * 512)).astype(BF16)


def _inproj(x, g, w_in, g_v):
    t = x.shape[0]
    tm = TOKEN_TILE
    row = lambda n: pl.BlockSpec((tm, n), lambda i: (i, 0))
    outs = [
        jax.ShapeDtypeStruct((t, A_WIDTH), BF16),
        jax.ShapeDtypeStruct((t, A_WIDTH), F32),
        jax.ShapeDtypeStruct((t, B_WIDTH), BF16),
        jax.ShapeDtypeStruct((t, B_WIDTH), F32),
        jax.ShapeDtypeStruct((t, B_WIDTH), F32),
        jax.ShapeDtypeStruct((t, C_WIDTH), F32),
        jax.ShapeDtypeStruct((t, C_WIDTH), BF16),
        jax.ShapeDtypeStruct((t, N_BRANCH * D_MODEL), BF16),
    ]
    return pl.pallas_call(
        _inproj_kernel,
        grid=(t // tm,),
        in_specs=[row(D_MODEL), _const_spec((1, D_MODEL)), _const_spec((D_MODEL, IN_COLS)),
                  _const_spec((1, A_WIDTH))],
        out_specs=[row(A_WIDTH), row(A_WIDTH), row(B_WIDTH), row(B_WIDTH), row(B_WIDTH),
                   row(C_WIDTH), row(C_WIDTH), row(N_BRANCH * D_MODEL)],
        out_shape=outs,
        compiler_params=_params("parallel"),
        name="inproj",
    )(x, g, w_in, g_v)


def _attend_tile(q_ref, kb, vt, bias_ref, bo_ref, q0, kblk0, nblk, bias_r0):
    tq = ATT_TQ
    nk = nblk * tq
    lane = lax.broadcasted_iota(jnp.int32, (tq, LANES), 1)
    orow = lax.broadcasted_iota(jnp.int32, (LANES, tq), 0)
    k0 = kblk0 * tq
    if not isinstance(k0, int):
        k0 = pl.multiple_of(k0, tq)

    def scores(h):
        cs = slice(h // 2 * LANES, (h // 2 + 1) * LANES)
        qp = q_ref[pl.ds(q0, tq), cs]
        sel = (lane < HEAD_DIM) if h % 2 == 0 else (lane >= HEAD_DIM)
        qh = jnp.where(sel, qp, jnp.zeros_like(qp))
        st = lax.dot_general(kb[pl.ds(k0, nk), cs], qh, (((1,), (1,)), ((), ())),
                             preferred_element_type=F32)
        return st + bias_ref[h, bias_r0:bias_r0 + nk, :]

    st_next = scores(0)
    outs = []
    for h in range(HEADS):
        cs = slice(h // 2 * LANES, (h // 2 + 1) * LANES)
        st = st_next
        if h + 1 < HEADS:
            st_next = scores(h + 1)
        m = jnp.max(st, axis=0, keepdims=True)
        eb = jnp.exp(st - m).astype(BF16)
        o = None
        for b in range(nblk):
            ob = jnp.dot(vt[h % 2, kblk0 + b, cs, :], eb[b * tq:(b + 1) * tq, :],
                         preferred_element_type=F32)
            o = ob if o is None else o + ob
        l = o[HEAD_DIM:HEAD_DIM + 1, :] if h % 2 == 0 else o[0:1, :]
        outs.append(o / l)
        if h % 2 == 1:
            ot = jnp.where(orow < HEAD_DIM, outs[0], outs[1])
            bo_ref[pl.ds(q0, tq), cs] = ot.T.astype(BF16)
            outs = []


def _attn_prompt_kernel(q_ref, k_ref, v_ref, bias_ref, bo_ref, kb, vt):
    s = q_ref.shape[0]
    tq = ATT_TQ
    nwin = REACH // tq + 1
    kb[...] = k_ref[...].astype(BF16)
    even_rows = lax.broadcasted_iota(jnp.int32, (B_WIDTH, tq), 0) % LANES < HEAD_DIM
    for jb in range(s // tq):
        vtr = v_ref[jb * tq:(jb + 1) * tq, :].T
        vt[0, jb] = jnp.where(even_rows, vtr, 1.0).astype(BF16)
        vt[1, jb] = jnp.where(even_rows, 1.0, vtr).astype(BF16)
    for t in range(nwin - 1):
        _attend_tile(q_ref, kb, vt, bias_ref, bo_ref, t * tq, 0, t + 1, (nwin - 1 - t) * tq)

    def tile(t, carry):
        _attend_tile(q_ref, kb, vt, bias_ref, bo_ref, pl.multiple_of(t * tq, tq),
                     t - (nwin - 1), nwin, 0)
        return carry

    lax.fori_loop(nwin - 1, s // tq, tile, 0)


def _attn_prompt(q, k, v, bias, nb, s):
    seq = lambda n: pl.BlockSpec((s, n), lambda b: (b, 0))
    tq = ATT_TQ
    return pl.pallas_call(
        _attn_prompt_kernel,
        grid=(nb,),
        in_specs=[seq(B_WIDTH), seq(B_WIDTH), seq(B_WIDTH),
                  _const_spec((HEADS, REACH + tq, tq))],
        out_specs=seq(B_WIDTH),
        out_shape=jax.ShapeDtypeStruct((nb * s, B_WIDTH), BF16),
        scratch_shapes=[pltpu.VMEM((s, B_WIDTH), BF16),
                        pltpu.VMEM((2, s // tq, B_WIDTH, tq), BF16)],
        compiler_params=_params("parallel"),
        name="attn_prompt",
    )(q, k, v, bias)


_ZPAD = 8


def _conv_rows(zext_ref, gb, cw_ref, n):
    z0 = zext_ref[pl.ds(_ZPAD, n), :]
    z1 = zext_ref[pl.ds(_ZPAD - 1, n), :]
    z2 = zext_ref[pl.ds(_ZPAD - 2, n), :]
    y = z2 * cw_ref[0:1, :] + z1 * cw_ref[1:2, :] + z0 * cw_ref[2:3, :]
    return (gb.astype(F32) * y).astype(BF16)


def _seq_sample_kernel(q_ref, k_ref, v_ref, ck_ref, cv_ref, zc_ref, gb_ref, cs_ref, bias_ref, cw_ref,
                       bo_ref, co_ref, ns_ref, kk, vv, zext):
    t = q_ref.shape[0]
    keep = ck_ref.shape[0]
    kk[0:keep, :] = ck_ref[...].astype(BF16)
    vv[0:keep, :] = cv_ref[...].astype(BF16)
    kk[keep:, :] = k_ref[...].astype(BF16)
    vv[keep:, :] = v_ref[...].astype(BF16)
    hist = CONV_WIDTH - 1
    zext[_ZPAD - hist:_ZPAD, :] = cs_ref[...]
    zext[_ZPAD:, :] = zc_ref[...]

    lane = lax.broadcasted_iota(jnp.int32, (t, LANES), 1)
    for p in range(HEADS // 2):
        cs = slice(p * LANES, (p + 1) * LANES)
        qp = q_ref[:, cs]
        kp = kk[:, cs]
        vp = vv[:, cs]
        outs = []
        for hh in range(2):
            sel = (lane < HEAD_DIM) if hh == 0 else (lane >= HEAD_DIM)
            qh = jnp.where(sel, qp, jnp.zeros_like(qp))
            s = lax.dot_general(qh, kp, (((1,), (1,)), ((), ())), preferred_element_type=F32)
            s = s + bias_ref[2 * p + hh]
            m = jnp.max(s, axis=-1, keepdims=True)
            e = jnp.exp(s - m)
            l = jnp.sum(e, axis=-1, keepdims=True)
            outs.append(jnp.dot(e.astype(BF16), vp, preferred_element_type=F32) / l)
        bo_ref[:, cs] = jnp.where(lane < HEAD_DIM, outs[0], outs[1]).astype(BF16)

    co_ref[...] = _conv_rows(zext, gb_ref[...], cw_ref, t)
    ns_ref[...] = zext[_ZPAD + t - hist:_ZPAD + t, :]


def _seq_sample(q, k, v, ck, cv, zc, gb, cs, bias, cw, nb, t):
    keep = ck.shape[1]
    hist = CONV_WIDTH - 1
    seq = lambda n: pl.BlockSpec((t, n), lambda b: (b, 0))
    per_b = lambda r, n: pl.BlockSpec((None, r, n), lambda b: (b, 0, 0))
    return pl.pallas_call(
        _seq_sample_kernel,
        grid=(nb,),
        in_specs=[seq(B_WIDTH), seq(B_WIDTH), seq(B_WIDTH), per_b(keep, B_WIDTH), per_b(keep, B_WIDTH),
                  seq(C_WIDTH), seq(C_WIDTH), per_b(hist, C_WIDTH),
                  _const_spec((HEADS, t, keep + t)), _const_spec((CONV_WIDTH, C_WIDTH))],
        out_specs=[seq(B_WIDTH), seq(C_WIDTH), per_b(hist, C_WIDTH)],
        out_shape=[jax.ShapeDtypeStruct((nb * t, B_WIDTH), BF16),
                   jax.ShapeDtypeStruct((nb * t, C_WIDTH), BF16),
                   jax.ShapeDtypeStruct((nb, hist, C_WIDTH), F32)],
        scratch_shapes=[pltpu.VMEM((keep + t, B_WIDTH), BF16),
                        pltpu.VMEM((keep + t, B_WIDTH), BF16),
                        pltpu.VMEM((_ZPAD + t, C_WIDTH), F32)],
        compiler_params=_params("parallel"),
        name="seq_sample",
    )(q, k, v, ck, cv, zc, gb, cs, bias, cw)


def _merge_body(ug_ref, vn_ref, bo_ref, co, sg_ref, x_ref, wm_ref, bsb_ref, wb_ref, wo_ref, o_ref, a_s):
    tm = ug_ref.shape[0]
    for n in range(tm // A_CHUNK):
        rows = slice(n * A_CHUNK, (n + 1) * A_CHUNK)
        for g in range(A_GROUPS):
            cols = slice(g * A_GROUP_DIM, (g + 1) * A_GROUP_DIM)
            mixed = jnp.dot(wm_ref[g], vn_ref[rows, cols].astype(BF16),
                            preferred_element_type=F32) + bsb_ref[g]
            a_s[rows, cols] = (ug_ref[rows, cols].astype(F32) * mixed).astype(BF16)
    m = None
    for n, br in enumerate((a_s[...], bo_ref[...], co)):
        p = jnp.dot(br, wb_ref[n], preferred_element_type=F32)
        term = sg_ref[:, n * D_MODEL:(n + 1) * D_MODEL].astype(F32) * p
        m = term if m is None else m + term
    y = jnp.dot(m.astype(BF16), wo_ref[...], preferred_element_type=F32)
    o_ref[...] = x_ref[...] + y


def _merge_conv_kernel(tiles_per_seq, ug_ref, vn_ref, bo_ref, zc_ref, halo_ref, gb_ref, sg_ref, x_ref,
                       wm_ref, bsb_ref, cw_ref, wb_ref, wo_ref, o_ref, a_s, zext):
    tm = ug_ref.shape[0]
    first = pl.program_id(0) % tiles_per_seq == 0
    zext[0:_ZPAD, :] = jnp.where(first, 0.0, halo_ref[...])
    zext[_ZPAD:, :] = zc_ref[...]
    co = _conv_rows(zext, gb_ref[...], cw_ref, tm)
    _merge_body(ug_ref, vn_ref, bo_ref, co, sg_ref, x_ref, wm_ref, bsb_ref, wb_ref, wo_ref, o_ref, a_s)


def _merge_kernel(ug_ref, vn_ref, bo_ref, co_ref, sg_ref, x_ref, wm_ref, bsb_ref, wb_ref, wo_ref,
                  o_ref, a_s):
    _merge_body(ug_ref, vn_ref, bo_ref, co_ref[...], sg_ref, x_ref, wm_ref, bsb_ref, wb_ref, wo_ref,
                o_ref, a_s)


def _merge(ug, vn, bo, sg, x, wm, bsb, wb, wo, *, co=None, zc=None, gb=None, cw=None, seq_len=None):
    t = x.shape[0]
    tm = TOKEN_TILE
    row = lambda n: pl.BlockSpec((tm, n), lambda i: (i, 0))
    weights = [_const_spec((A_GROUPS, A_CHUNK, A_CHUNK)), _const_spec((A_GROUPS, A_CHUNK, A_GROUP_DIM))]
    tail = [_const_spec((N_BRANCH, A_WIDTH, D_MODEL)), _const_spec((D_MODEL, D_MODEL))]
    scratch = [pltpu.VMEM((tm, A_WIDTH), BF16)]
    if co is not None:
        body = _merge_kernel
        in_specs = [row(A_WIDTH), row(A_WIDTH), row(B_WIDTH), row(C_WIDTH), row(N_BRANCH * D_MODEL),
                    row(D_MODEL)] + weights + tail
        args = (ug, vn, bo, co, sg, x, wm, bsb, wb, wo)
    else:
        assert seq_len % tm == 0
        body = functools.partial(_merge_conv_kernel, seq_len // tm)
        per = tm // _ZPAD
        halo = pl.BlockSpec((_ZPAD, C_WIDTH), lambda i: (jnp.maximum(i * per - 1, 0), 0))
        in_specs = [row(A_WIDTH), row(A_WIDTH), row(B_WIDTH), row(C_WIDTH), halo, row(C_WIDTH),
                    row(N_BRANCH * D_MODEL), row(D_MODEL)] + weights + [
                        _const_spec((CONV_WIDTH, C_WIDTH))] + tail
        args = (ug, vn, bo, zc, zc, gb, sg, x, wm, bsb, cw, wb, wo)
        scratch = scratch + [pltpu.VMEM((_ZPAD + tm, C_WIDTH), F32)]
    return pl.pallas_call(
        body,
        grid=(t // tm,),
        in_specs=in_specs,
        out_specs=row(D_MODEL),
        out_shape=jax.ShapeDtypeStruct((t, D_MODEL), F32),
        scratch_shapes=scratch,
        compiler_params=_params("parallel"),
        name="merge",
    )(*args)


_FF_SPLITS = ((0, 1280), (1280, 1536))


def _ffn_kernel(x_ref, g_ref, w1_ref, w3_ref, w2_ref, o_ref):
    x = x_ref[...]
    h = _rms(x, g_ref[...]).astype(BF16)
    y = x
    for a, n in _FF_SPLITS:
        u = jnp.dot(h, w1_ref[:, a:a + n], preferred_element_type=F32)
        g = jnp.dot(h, w3_ref[:, a:a + n], preferred_element_type=F32)
        mid = (jax.nn.silu(u) * g).astype(BF16)
        y = y + jnp.dot(mid, w2_ref[a:a + n, :], preferred_element_type=F32)
    o_ref[...] = y


def _ffn(x, g, w1, w3, w2):
    t = x.shape[0]
    tm = TOKEN_TILE
    row = pl.BlockSpec((tm, D_MODEL), lambda i: (i, 0))
    return pl.pallas_call(
        _ffn_kernel,
        grid=(t // tm,),
        in_specs=[row, _const_spec((1, D_MODEL)), _const_spec((D_MODEL, D_FF)),
                  _const_spec((D_MODEL, D_FF)), _const_spec((D_FF, D_MODEL))],
        out_specs=row,
        out_shape=jax.ShapeDtypeStruct((t, D_MODEL), F32),
        compiler_params=_params("parallel"),
        name="ffn",
    )(x, g, w1, w3, w2)


_M_E1, _M_E2, _M_R1, _M_R2, _M_W1, _M_W2 = range(6)


def _pair_specs(n_first):
    tm = TOKEN_TILE
    return (pl.BlockSpec((tm, D_MODEL), lambda i: (jnp.minimum(i, n_first - 1), 0)),
            pl.BlockSpec((tm, D_MODEL), lambda i: (jnp.maximum(i - n_first, 0), 0)))


def _pair_tiles(xa, xb):
    tm = TOKEN_TILE
    assert xa.shape[0] % tm == 0 and xb.shape[0] % tm == 0 and xa.shape[0] > 0 and xb.shape[0] > 0
    return xa.shape[0] // tm, (xa.shape[0] + xb.shape[0]) // tm


def _router_kernel(n_first, xa_ref, xb_ref, g_ref, wrt_ref, br_ref, tri_ref, meta_ref, cnt_ref, cnt_s):
    @pl.when(pl.program_id(0) == 0)
    def _():
        cnt_s[...] = jnp.zeros_like(cnt_s)

    x = jnp.where(pl.program_id(0) < n_first, xa_ref[...], xb_ref[...])
    h = _rms(x, g_ref[...])
    logits = lax.dot_general(wrt_ref[...], h, (((1,), (1,)), ((), ())), preferred_element_type=F32,
                             precision=lax.Precision.HIGHEST) + br_ref[...]
    eid = lax.broadcasted_iota(jnp.int32, logits.shape, 0)
    m1 = jnp.max(logits, axis=0, keepdims=True)
    i1 = jnp.min(jnp.where(logits == m1, eid, N_EXPERTS), axis=0, keepdims=True)
    rest = jnp.where(eid == i1, NEG_INF, logits)
    m2 = jnp.max(rest, axis=0, keepdims=True)
    i2 = jnp.min(jnp.where(rest == m2, eid, N_EXPERTS), axis=0, keepdims=True)
    e2 = jnp.exp(m2 - m1)
    den = 1.0 + e2
    oh1 = (eid == i1).astype(F32)
    oh2 = (eid == i2).astype(F32)
    both = oh1 + oh2
    before = jnp.dot(both.astype(BF16), tri_ref[...], preferred_element_type=F32) + cnt_s[:, 0:1]
    meta_ref[_M_E1:_M_E1 + 1, :] = i1.astype(F32)
    meta_ref[_M_E2:_M_E2 + 1, :] = i2.astype(F32)
    meta_ref[_M_R1:_M_R1 + 1, :] = jnp.sum(oh1 * before, axis=0, keepdims=True)
    meta_ref[_M_R2:_M_R2 + 1, :] = jnp.sum(oh2 * before, axis=0, keepdims=True)
    meta_ref[_M_W1:_M_W1 + 1, :] = 1.0 / den
    meta_ref[_M_W2:_M_W2 + 1, :] = e2 / den
    meta_ref[_M_W2 + 1:, :] = jnp.zeros((SUBLANES - _M_W2 - 1, logits.shape[1]), F32)
    cnt_s[...] = cnt_s[...] + jnp.sum(both, axis=1, keepdims=True)
    cnt_ref[...] = cnt_s[...]


def _router(xa, xb, g, wrt, br):
    tm = TOKEN_TILE
    n_first, nt = _pair_tiles(xa, xb)
    tri = jnp.asarray(np.triu(np.ones((tm, tm), np.float32), 1), BF16)
    return pl.pallas_call(
        functools.partial(_router_kernel, n_first),
        grid=(nt,),
        in_specs=[*_pair_specs(n_first), _const_spec((1, D_MODEL)),
                  _const_spec((N_EXPERTS, D_MODEL)), _const_spec((N_EXPERTS, 1)), _const_spec((tm, tm))],
        out_specs=[pl.BlockSpec((None, SUBLANES, tm), lambda i: (i, 0, 0)),
                   pl.BlockSpec((N_EXPERTS, LANES), lambda i: (0, 0))],
        out_shape=[jax.ShapeDtypeStruct((nt, SUBLANES, tm), F32),
                   jax.ShapeDtypeStruct((N_EXPERTS, LANES), F32)],
        scratch_shapes=[pltpu.VMEM((N_EXPERTS, LANES), F32)],
        compiler_params=_params("arbitrary"),
        name="router",
    )(xa, xb, g, wrt, br, tri)


def _row_copy(src, s, dst, d, sem):
    return pltpu.make_async_copy(src.at[pl.ds(s, 1)], dst.at[pl.ds(d, 1)], sem)


def _dispatch_kernel(n_first, slots_ref, xa_ref, xb_ref, xs_in, xs_hbm, sem):
    del xs_in
    tm = TOKEN_TILE

    def scatter(x_ref):
        def issue(r, c):
            for k in range(TOP_K):
                _row_copy(x_ref, r, xs_hbm, slots_ref[0, k * tm + r], sem).start()
            return c

        lax.fori_loop(0, tm, issue, 0, unroll=8)

        def drain(r, c):
            for k in range(TOP_K):
                _row_copy(x_ref, r, xs_hbm, slots_ref[0, k * tm + r], sem).wait()
            return c

        lax.fori_loop(0, tm, drain, 0, unroll=8)

    @pl.when(pl.program_id(0) < n_first)
    def _():
        scatter(xa_ref)

    @pl.when(pl.program_id(0) >= n_first)
    def _():
        scatter(xb_ref)


def _dispatch(slots, xa, xb, n_rows):
    tm = TOKEN_TILE
    n_first, nt = _pair_tiles(xa, xb)
    xs0 = jnp.zeros((n_rows, D_MODEL), F32)
    return pl.pallas_call(
        functools.partial(_dispatch_kernel, n_first),
        grid=(nt,),
        in_specs=[pl.BlockSpec((None, 1, TOP_K * tm), lambda i: (i, 0, 0), memory_space=pltpu.SMEM),
                  *_pair_specs(n_first), pl.BlockSpec(memory_space=pl.ANY)],
        out_specs=pl.BlockSpec(memory_space=pl.ANY),
        out_shape=jax.ShapeDtypeStruct((n_rows, D_MODEL), F32),
        scratch_shapes=[pltpu.SemaphoreType.DMA(())],
        input_output_aliases={3: 0},
        compiler_params=_params("arbitrary"),
        name="dispatch",
    )(slots, xa, xb, xs0)


_E_COLS = 512


def _expert_kernel(te_ref, nu_ref, xs_ref, g_ref, w1_ref, w3_ref, w2_ref, ys_ref):
    i = pl.program_id(0)

    @pl.when(i < nu_ref[0])
    def _():
        h = _rms(xs_ref[...], g_ref[...]).astype(BF16)
        y = None
        for c in range(D_FF_EXPERT // _E_COLS):
            cols = slice(c * _E_COLS, (c + 1) * _E_COLS)
            u = jnp.dot(h, w1_ref[:, cols], preferred_element_type=F32)
            g = jnp.dot(h, w3_ref[:, cols], preferred_element_type=F32)
            mid = (jax.nn.silu(u) * g).astype(BF16)
            part = jnp.dot(mid, w2_ref[cols, :], preferred_element_type=F32)
            y = part if y is None else y + part
        ys_ref[...] = y

    @pl.when(i >= nu_ref[0])
    def _():
        ys_ref[...] = jnp.zeros_like(ys_ref)


def _experts(tile_expert, n_used, xs, g, w1, w3, w2):
    n_rows = xs.shape[0]
    tm = EXPERT_TILE
    row = pl.BlockSpec((tm, D_MODEL), lambda i, te, nu: (i, 0))
    wspec = lambda r, c: pl.BlockSpec((None, r, c), lambda i, te, nu: (te[i], 0, 0),
                                      pipeline_mode=pl.Buffered(1))
    return pl.pallas_call(
        _expert_kernel,
        grid_spec=pltpu.PrefetchScalarGridSpec(
            num_scalar_prefetch=2,
            grid=(n_rows // tm,),
            in_specs=[row, pl.BlockSpec((1, D_MODEL), lambda i, te, nu: (0, 0),
                                        pipeline_mode=pl.Buffered(1)),
                      wspec(D_MODEL, D_FF_EXPERT), wspec(D_MODEL, D_FF_EXPERT),
                      wspec(D_FF_EXPERT, D_MODEL)],
            out_specs=row,
        ),
        out_shape=jax.ShapeDtypeStruct((n_rows, D_MODEL), F32),
        compiler_params=_params("arbitrary"),
        name="experts",
    )(tile_expert, n_used, xs, g, w1, w3, w2)


def _combine_kernel(n_first, slots_ref, meta_ref, xa_ref, xb_ref, ys_hbm, gf_ref, oa_ref, ob_ref,
                    rows, sem):
    tm = TOKEN_TILE

    def issue(r, c):
        for k in range(TOP_K):
            _row_copy(ys_hbm, slots_ref[0, k * tm + r], rows.at[k], r, sem).start()
        return c

    lax.fori_loop(0, tm, issue, 0, unroll=8)

    def drain(r, c):
        for k in range(TOP_K):
            _row_copy(ys_hbm, slots_ref[0, k * tm + r], rows.at[k], r, sem).wait()
        return c

    lax.fori_loop(0, tm, drain, 0, unroll=8)

    meta = jnp.concatenate([meta_ref[...], jnp.zeros((LANES - SUBLANES, tm), F32)], axis=0)
    mt = meta.T
    w1 = mt[:, _M_W1:_M_W1 + 1]
    w2 = mt[:, _M_W2:_M_W2 + 1]
    i = pl.program_id(0)
    x = jnp.where(i < n_first, xa_ref[...], xb_ref[...])
    y = _rms(x + w1 * rows[0] + w2 * rows[1], gf_ref[...])

    @pl.when(i < n_first)
    def _():
        oa_ref[...] = y

    @pl.when(i >= n_first)
    def _():
        ob_ref[...] = y


def _combine(slots, meta, xa, xb, ys, gf):
    tm = TOKEN_TILE
    n_first, nt = _pair_tiles(xa, xb)
    return pl.pallas_call(
        functools.partial(_combine_kernel, n_first),
        grid=(nt,),
        in_specs=[pl.BlockSpec((None, 1, TOP_K * tm), lambda i: (i, 0, 0), memory_space=pltpu.SMEM),
                  pl.BlockSpec((None, SUBLANES, tm), lambda i: (i, 0, 0)),
                  *_pair_specs(n_first), pl.BlockSpec(memory_space=pl.ANY), _const_spec((1, D_MODEL))],
        out_specs=list(_pair_specs(n_first)),
        out_shape=[jax.ShapeDtypeStruct(xa.shape, F32), jax.ShapeDtypeStruct(xb.shape, F32)],
        scratch_shapes=[pltpu.VMEM((TOP_K, tm, D_MODEL), F32), pltpu.SemaphoreType.DMA(())],
        compiler_params=_params("arbitrary"),
        name="combine",
    )(slots, meta, xa, xb, ys, gf)


def _moe_final(xa, xb, g, wr, br, w1, w3, w2, gf):
    t = xa.shape[0] + xb.shape[0]
    tm = EXPERT_TILE
    n_tiles = (TOP_K * t + N_EXPERTS * (tm - 1) + tm - 1) // tm
    meta, cnt = _router(xa, xb, g, wr.T, br.reshape(N_EXPERTS, 1))

    counts = cnt[:, 0].astype(jnp.int32)
    tiles = (counts + tm - 1) // tm
    tile_end = jnp.cumsum(tiles)
    row_off = (tile_end - tiles) * tm
    n_used = tile_end[-1]
    tid = jnp.minimum(jnp.arange(n_tiles, dtype=jnp.int32), n_used - 1)
    tile_expert = jnp.sum(tid[:, None] >= tile_end[None, :], axis=1).astype(jnp.int32)

    def slot(e_row, r_row):
        e = meta[:, e_row, :].astype(jnp.int32)
        off = jnp.sum(jnp.where(e[..., None] == jnp.arange(N_EXPERTS), row_off, 0), axis=-1)
        return off + meta[:, r_row, :].astype(jnp.int32)

    slots = jnp.concatenate([slot(_M_E1, _M_R1), slot(_M_E2, _M_R2)], axis=1)[:, None, :]

    xs = _dispatch(slots, xa, xb, n_tiles * tm)
    ys = _experts(tile_expert, n_used.reshape(1), xs, g, w1, w3, w2)
    return _combine(slots, meta, xa, xb, ys, gf)


def _rel_bias_t(table, n_q, n_k, offset):
    h = table.shape[0]
    span = n_q + n_k - 1
    dist = offset + np.arange(span) - (n_k - 1)
    vec = table[:, np.clip(dist, -REL_CLIP, REL_CLIP) + REL_CLIP].astype(F32)
    wide = jnp.broadcast_to(jnp.pad(vec, ((0, 0), (0, 1)))[:, None, :], (h, n_k, span + 1))
    skew = wide.reshape(h, n_k * (span + 1))[:, :n_k * span].reshape(h, n_k, span)
    return skew[:, :, n_k - 1:n_k - 1 + n_q]


def _prompt_bias(table):
    tq = ATT_TQ
    bias = _rel_bias_t(table, tq, REACH + tq, REACH)
    kc = np.arange(REACH + tq)[:, None] // CHUNK
    qc = np.arange(tq)[None, :] // CHUNK
    band = (kc >= qc) & (kc <= qc + REACH // CHUNK)
    return jnp.where(band[None], bias, NEG_INF)


def _gmlp_mats(ws, bs, lc):
    pos = np.arange(lc)
    mask = (pos[None, :] // CHUNK) <= (pos[:, None] // CHUNK)
    w = jnp.where(mask[None], ws[:, :lc, :lc], 0)
    reps = A_CHUNK // lc
    if reps > 1:
        eye = jnp.eye(reps, dtype=w.dtype)
        w = jnp.einsum('rs,gij->grisj', eye, w).reshape(A_GROUPS, A_CHUNK, A_CHUNK)
    b = jnp.tile(bs[:, :lc], (1, reps))
    bsb = jnp.broadcast_to(b[:, :, None], (A_GROUPS, A_CHUNK, A_GROUP_DIM))
    return w.astype(BF16), bsb.astype(F32)


def kernel(x_prompt, x_sample, cache_attn_k, cache_attn_v, state_conv, norm_mix_g, w_in, gmlp_norm_g, gmlp_ws, gmlp_bs, attn_rel_bias, conv_w, w_branch, w_out, norm_ffn_g, ffn_w1, ffn_w3, ffn_w2, moe_router, moe_router_b, moe_w1, moe_w3, moe_w2, final_norm_g):
    nb, s, d = x_prompt.shape
    nbs, t, _ = x_sample.shape
    keep = cache_attn_k.shape[2]
    kv_keep = min(REACH, s)
    xp = x_prompt.reshape(nb * s, d)
    xs = x_sample.reshape(nbs * t, d)
    gf = final_norm_g.reshape(1, d)

    pk, pv, pc, sk, sv, sc, sg_out = [], [], [], [], [], [], []
    for l in range(DEPTH):
        g_mix = norm_mix_g[l].reshape(1, d)
        g_v = gmlp_norm_g[l].reshape(1, A_WIDTH)
        w_in_b = w_in[l].astype(BF16)
        wb_b = w_branch[l].astype(BF16)
        wo_b = w_out[l].astype(BF16)
        cw = conv_w[l]
        bias_p = _prompt_bias(attn_rel_bias[l])
        bias_s = jnp.swapaxes(_rel_bias_t(attn_rel_bias[l], t, keep + t, keep), 1, 2)
        wm_p, bsb_p = _gmlp_mats(gmlp_ws[l], gmlp_bs[l], A_CHUNK)
        wm_s, bsb_s = _gmlp_mats(gmlp_ws[l], gmlp_bs[l], t)

        ug, vn, q, k, v, zc, gb, sgt = _inproj(xp, g_mix, w_in_b, g_v)
        bo = _attn_prompt(q, k, v, bias_p, nb, s)
        xp = _merge(ug, vn, bo, sgt, xp, wm_p, bsb_p, wb_b, wo_b, zc=zc, gb=gb, cw=cw, seq_len=s)
        pk.append(k.reshape(nb, s, HEADS, HEAD_DIM)[:, s - kv_keep:])
        pv.append(v.reshape(nb, s, HEADS, HEAD_DIM)[:, s - kv_keep:])
        pc.append(zc.reshape(nb, s, C_WIDTH)[:, s - (CONV_WIDTH - 1):])

        ug, vn, q, k, v, zc, gb, sgt = _inproj(xs, g_mix, w_in_b, g_v)
        ck = cache_attn_k[l].reshape(nbs, keep, B_WIDTH)
        cv = cache_attn_v[l].reshape(nbs, keep, B_WIDTH)
        bo, co, ns = _seq_sample(q, k, v, ck, cv, zc, gb, state_conv[l], bias_s, cw, nbs, t)
        xs = _merge(ug, vn, bo, sgt, xs, wm_s, bsb_s, wb_b, wo_b, co=co)
        sk.append(k.reshape(nbs, t, HEADS, HEAD_DIM))
        sv.append(v.reshape(nbs, t, HEADS, HEAD_DIM))
        sc.append(ns)
        sg_out.append(vn.reshape(nbs, t, A_WIDTH))

        g_ffn = norm_ffn_g[l].reshape(1, d)
        i = l // 2
        if l % 2 == 0:
            w1 = ffn_w1[i].astype(BF16)
            w3 = ffn_w3[i].astype(BF16)
            w2 = ffn_w2[i].astype(BF16)
            xp = _ffn(xp, g_ffn, w1, w3, w2)
            xs = _ffn(xs, g_ffn, w1, w3, w2)
        else:
            assert l == DEPTH - 1
            xp, xs = _moe_final(xp, xs, g_ffn, moe_router[i], moe_router_b[i], moe_w1[i].astype(BF16),
                                moe_w3[i].astype(BF16), moe_w2[i].astype(BF16), gf)

    y_prompt = xp.reshape(nb, s, d)
    y_sample = xs.reshape(nbs, t, d)
    return (y_prompt, y_sample, jnp.stack(pk), jnp.stack(pv), jnp.stack(pc),
            jnp.stack(sk), jnp.stack(sv), jnp.stack(sc), jnp.stack(sg_out))
```

```python
import functools

import jax
import jax.numpy as jnp
import numpy as np
from jax import lax
from jax.experimental import pallas as pl
from jax.experimental.pallas import tpu as pltpu

F32 = jnp.float32
BF16 = jnp.bfloat16

D_MODEL = 1024
DEPTH = 2
CHUNK = 64
A_CHUNK = 128
A_WIDTH = 512
A_GROUPS = 4
A_GROUP_DIM = 128
HEAD_DIM = 64
HEADS = 8
B_WIDTH = 512
REACH = 512
REL_CLIP = 128
C_WIDTH = 512
CONV_WIDTH = 3
N_BRANCH = 3
D_FF = 2816
N_EXPERTS = 8
TOP_K = 2
D_FF_EXPERT = 3584
EPS = 1e-6
NEG_INF = -1e30

_O_UA, _O_VA, _O_Q, _O_K, _O_V, _O_CIN, _O_GB, _O_GC, _O_GATES = (
    0, 512, 1024, 1536, 2048, 2560, 3072, 3584, 4096)
IN_COLS = 7168

TOKEN_TILE = 512
EXPERT_TILE = 512
ATT_TQ = 256
VMEM_LIMIT = 56 * 1024 * 1024
LANES = 128
SUBLANES = 8


def _params(*sem):
    return pltpu.CompilerParams(dimension_semantics=sem, vmem_limit_bytes=VMEM_LIMIT)


def _const_spec(shape):
    nd = len(shape)
    return pl.BlockSpec(shape, lambda *_: (0,) * nd, pipeline_mode=pl.Buffered(1))


def _rms(x, g):
    ms = jnp.mean(x * x, axis=-1, keepdims=True)
    return x * lax.rsqrt(ms + EPS) * g


def _inproj_kernel(x_ref, g_ref, w_ref, gv_ref,
                   ug_ref, vn_ref, q_ref, k_ref, v_ref, zc_ref, gb_ref, sg_ref):
    h = _rms(x_ref[...], g_ref[...]).astype(BF16)

    def col(a, n=512):
        return jnp.dot(h, w_ref[:, a:a + n], preferred_element_type=F32)

    ug_ref[...] = jax.nn.gelu(col(_O_UA)).astype(BF16)
    vn_ref[...] = _rms(jax.nn.gelu(col(_O_VA)), gv_ref[...])
    q_ref[...] = (col(_O_Q) * (HEAD_DIM ** -0.5)).astype(BF16)
    k_ref[...] = col(_O_K)
    v_ref[...] = col(_O_V)
    zc_ref[...] = col(_O_GC) * col(_O_CIN)
    gb_ref[...] = col(_O_GB).astype(BF16)
    for j in range(N_BRANCH * D_MODEL // 512):
        sg_ref[:, j * 512:(j + 1) * 512] = jax.nn.sigmoid(col(_O_GATES + j * 512)).astype(BF16)


def _inproj(x, g, w_in, g_v):
    t = x.shape[0]
    tm = TOKEN_TILE
    row = lambda n: pl.BlockSpec((tm, n), lambda i: (i, 0))
    outs = [
        jax.ShapeDtypeStruct((t, A_WIDTH), BF16),
        jax.ShapeDtypeStruct((t, A_WIDTH), F32),
        jax.ShapeDtypeStruct((t, B_WIDTH), BF16),
        jax.ShapeDtypeStruct((t, B_WIDTH), F32),
        jax.ShapeDtypeStruct((t, B_WIDTH), F32),
        jax.ShapeDtypeStruct((t, C_WIDTH), F32),
        jax.ShapeDtypeStruct((t, C_WIDTH), BF16),
        jax.ShapeDtypeStruct((t, N_BRANCH * D_MODEL), BF16),
    ]
    return pl.pallas_call(
        _inproj_kernel,
        grid=(t // tm,),
        in_specs=[row(D_MODEL), _const_spec((1, D_MODEL)), _const_spec((D_MODEL, IN_COLS)),
                  _const_spec((1, A_WIDTH))],
        out_specs=[row(A_WIDTH), row(A_WIDTH), row(B_WIDTH), row(B_WIDTH), row(B_WIDTH),
                   row(C_WIDTH), row(C_WIDTH), row(N_BRANCH * D_MODEL)],
        out_shape=outs,
        compiler_params=_params("parallel"),
        name="inproj",
    )(x, g, w_in, g_v)


_ATT_AHEAD = 3


def _attend_tile(q_ref, kb, vt, bias_ref, bo_ref, q0, kblk0, nblk, bias_r0):
    tq = ATT_TQ
    nk = nblk * tq
    lane = lax.broadcasted_iota(jnp.int32, (tq, LANES), 1)
    orow = lax.broadcasted_iota(jnp.int32, (LANES, tq), 0)
    k0 = kblk0 * tq
    if not isinstance(k0, int):
        k0 = pl.multiple_of(k0, tq)

    def scores(h):
        cs = slice(h // 2 * LANES, (h // 2 + 1) * LANES)
        qp = q_ref[pl.ds(q0, tq), cs]
        sel = (lane < HEAD_DIM) if h % 2 == 0 else (lane >= HEAD_DIM)
        qh = jnp.where(sel, qp, jnp.zeros_like(qp))
        st = lax.dot_general(kb[pl.ds(k0, nk), cs], qh, (((1,), (1,)), ((), ())),
                             preferred_element_type=F32)
        return st + bias_ref[h, bias_r0:bias_r0 + nk, :]

    ahead = [scores(h) for h in range(_ATT_AHEAD)]
    outs = []
    for h in range(HEADS):
        cs = slice(h // 2 * LANES, (h // 2 + 1) * LANES)
        st = ahead.pop(0)
        if h + _ATT_AHEAD < HEADS:
            ahead.append(scores(h + _ATT_AHEAD))
        m = jnp.max(st, axis=0, keepdims=True)
        eb = jnp.exp(st - m).astype(BF16)
        o = None
        for b in range(nblk):
            ob = jnp.dot(vt[h % 2, kblk0 + b, cs, :], eb[b * tq:(b + 1) * tq, :],
                         preferred_element_type=F32)
            o = ob if o is None else o + ob
        l = o[HEAD_DIM:HEAD_DIM + 1, :] if h % 2 == 0 else o[0:1, :]
        outs.append(o / l)
        if h % 2 == 1:
            ot = jnp.where(orow < HEAD_DIM, outs[0], outs[1])
            bo_ref[pl.ds(q0, tq), cs] = ot.T.astype(BF16)
            outs = []


def _attn_prompt_kernel(q_ref, k_ref, v_ref, bias_ref, bo_ref, kb, vt):
    s = q_ref.shape[0]
    tq = ATT_TQ
    nwin = REACH // tq + 1
    kb[...] = k_ref[...].astype(BF16)
    even_rows = lax.broadcasted_iota(jnp.int32, (B_WIDTH, tq), 0) % LANES < HEAD_DIM
    for jb in range(s // tq):
        vtr = v_ref[jb * tq:(jb + 1) * tq, :].T
        vt[0, jb] = jnp.where(even_rows, vtr, 1.0).astype(BF16)
        vt[1, jb] = jnp.where(even_rows, 1.0, vtr).astype(BF16)
    for t in range(nwin - 1):
        _attend_tile(q_ref, kb, vt, bias_ref, bo_ref, t * tq, 0, t + 1, (nwin - 1 - t) * tq)

    def tile(t, carry):
        _attend_tile(q_ref, kb, vt, bias_ref, bo_ref, pl.multiple_of(t * tq, tq),
                     t - (nwin - 1), nwin, 0)
        return carry

    lax.fori_loop(nwin - 1, s // tq, tile, 0)


def _attn_prompt(q, k, v, bias, nb, s):
    seq = lambda n: pl.BlockSpec((s, n), lambda b: (b, 0))
    tq = ATT_TQ
    return pl.pallas_call(
        _attn_prompt_kernel,
        grid=(nb,),
        in_specs=[seq(B_WIDTH), seq(B_WIDTH), seq(B_WIDTH),
                  _const_spec((HEADS, REACH + tq, tq))],
        out_specs=seq(B_WIDTH),
        out_shape=jax.ShapeDtypeStruct((nb * s, B_WIDTH), BF16),
        scratch_shapes=[pltpu.VMEM((s, B_WIDTH), BF16),
                        pltpu.VMEM((2, s // tq, B_WIDTH, tq), BF16)],
        compiler_params=_params("parallel"),
        name="attn_prompt",
    )(q, k, v, bias)


_ZPAD = 8


def _conv_rows(zext_ref, gb, cw_ref, n):
    z0 = zext_ref[pl.ds(_ZPAD, n), :]
    z1 = zext_ref[pl.ds(_ZPAD - 1, n), :]
    z2 = zext_ref[pl.ds(_ZPAD - 2, n), :]
    y = z2 * cw_ref[0:1, :] + z1 * cw_ref[1:2, :] + z0 * cw_ref[2:3, :]
    return (gb.astype(F32) * y).astype(BF16)


def _seq_sample_kernel(q_ref, k_ref, v_ref, ck_ref, cv_ref, zc_ref, gb_ref, cs_ref, bias_ref, cw_ref,
                       bo_ref, co_ref, ns_ref, kk, vv, zext):
    t = q_ref.shape[0]
    keep = ck_ref.shape[0]
    kk[0:keep, :] = ck_ref[...].astype(BF16)
    vv[0:keep, :] = cv_ref[...].astype(BF16)
    kk[keep:, :] = k_ref[...].astype(BF16)
    vv[keep:, :] = v_ref[...].astype(BF16)
    hist = CONV_WIDTH - 1
    zext[_ZPAD - hist:_ZPAD, :] = cs_ref[...]
    zext[_ZPAD:, :] = zc_ref[...]

    lane = lax.broadcasted_iota(jnp.int32, (t, LANES), 1)
    for p in range(HEADS // 2):
        cs = slice(p * LANES, (p + 1) * LANES)
        qp = q_ref[:, cs]
        kp = kk[:, cs]
        vp = vv[:, cs]
        outs = []
        for hh in range(2):
            sel = (lane < HEAD_DIM) if hh == 0 else (lane >= HEAD_DIM)
            qh = jnp.where(sel, qp, jnp.zeros_like(qp))
            s = lax.dot_general(qh, kp, (((1,), (1,)), ((), ())), preferred_element_type=F32)
            s = s + bias_ref[2 * p + hh]
            m = jnp.max(s, axis=-1, keepdims=True)
            e = jnp.exp(s - m)
            l = jnp.sum(e, axis=-1, keepdims=True)
            outs.append(jnp.dot(e.astype(BF16), vp, preferred_element_type=F32) / l)
        bo_ref[:, cs] = jnp.where(lane < HEAD_DIM, outs[0], outs[1]).astype(BF16)

    co_ref[...] = _conv_rows(zext, gb_ref[...], cw_ref, t)
    ns_ref[...] = zext[_ZPAD + t - hist:_ZPAD + t, :]


def _seq_sample(q, k, v, ck, cv, zc, gb, cs, bias, cw, nb, t):
    keep = ck.shape[1]
    hist = CONV_WIDTH - 1
    seq = lambda n: pl.BlockSpec((t, n), lambda b: (b, 0))
    per_b = lambda r, n: pl.BlockSpec((None, r, n), lambda b: (b, 0, 0))
    return pl.pallas_call(
        _seq_sample_kernel,
        grid=(nb,),
        in_specs=[seq(B_WIDTH), seq(B_WIDTH), seq(B_WIDTH), per_b(keep, B_WIDTH), per_b(keep, B_WIDTH),
                  seq(C_WIDTH), seq(C_WIDTH), per_b(hist, C_WIDTH),
                  _const_spec((HEADS, t, keep + t)), _const_spec((CONV_WIDTH, C_WIDTH))],
        out_specs=[seq(B_WIDTH), seq(C_WIDTH), per_b(hist, C_WIDTH)],
        out_shape=[jax.ShapeDtypeStruct((nb * t, B_WIDTH), BF16),
                   jax.ShapeDtypeStruct((nb * t, C_WIDTH), BF16),
                   jax.ShapeDtypeStruct((nb, hist, C_WIDTH), F32)],
        scratch_shapes=[pltpu.VMEM((keep + t, B_WIDTH), BF16),
                        pltpu.VMEM((keep + t, B_WIDTH), BF16),
                        pltpu.VMEM((_ZPAD + t, C_WIDTH), F32)],
        compiler_params=_params("parallel"),
        name="seq_sample",
    )(q, k, v, ck, cv, zc, gb, cs, bias, cw)


def _merge_body(ug_ref, vn_ref, bo_ref, co, sg_ref, x_ref, wm_ref, bsb_ref, wb_ref, wo_ref, o_ref, a_s):
    tm = ug_ref.shape[0]
    for n in range(tm // A_CHUNK):
        rows = slice(n * A_CHUNK, (n + 1) * A_CHUNK)
        for g in range(A_GROUPS):
            cols = slice(g * A_GROUP_DIM, (g + 1) * A_GROUP_DIM)
            mixed = jnp.dot(wm_ref[g], vn_ref[rows, cols].astype(BF16),
                            preferred_element_type=F32) + bsb_ref[g]
            a_s[rows, cols] = (ug_ref[rows, cols].astype(F32) * mixed).astype(BF16)
    m = None
    for n, br in enumerate((a_s[...], bo_ref[...], co)):
        p = jnp.dot(br, wb_ref[n], preferred_element_type=F32)
        term = sg_ref[:, n * D_MODEL:(n + 1) * D_MODEL].astype(F32) * p
        m = term if m is None else m + term
    y = jnp.dot(m.astype(BF16), wo_ref[...], preferred_element_type=F32)
    o_ref[...] = x_ref[...] + y


def _merge_conv_kernel(tiles_per_seq, ug_ref, vn_ref, bo_ref, zc_ref, halo_ref, gb_ref, sg_ref, x_ref,
                       wm_ref, bsb_ref, cw_ref, wb_ref, wo_ref, o_ref, a_s, zext):
    tm = ug_ref.shape[0]
    first = pl.program_id(0) % tiles_per_seq == 0
    zext[0:_ZPAD, :] = jnp.where(first, 0.0, halo_ref[...])
    zext[_ZPAD:, :] = zc_ref[...]
    co = _conv_rows(zext, gb_ref[...], cw_ref, tm)
    _merge_body(ug_ref, vn_ref, bo_ref, co, sg_ref, x_ref, wm_ref, bsb_ref, wb_ref, wo_ref, o_ref, a_s)


def _merge_kernel(ug_ref, vn_ref, bo_ref, co_ref, sg_ref, x_ref, wm_ref, bsb_ref, wb_ref, wo_ref,
                  o_ref, a_s):
    _merge_body(ug_ref, vn_ref, bo_ref, co_ref[...], sg_ref, x_ref, wm_ref, bsb_ref, wb_ref, wo_ref,
                o_ref, a_s)


def _merge(ug, vn, bo, sg, x, wm, bsb, wb, wo, *, co=None, zc=None, gb=None, cw=None, seq_len=None):
    t = x.shape[0]
    tm = TOKEN_TILE
    row = lambda n: pl.BlockSpec((tm, n), lambda i: (i, 0))
    weights = [_const_spec((A_GROUPS, A_CHUNK, A_CHUNK)), _const_spec((A_GROUPS, A_CHUNK, A_GROUP_DIM))]
    tail = [_const_spec((N_BRANCH, A_WIDTH, D_MODEL)), _const_spec((D_MODEL, D_MODEL))]
    scratch = [pltpu.VMEM((tm, A_WIDTH), BF16)]
    if co is not None:
        body = _merge_kernel
        in_specs = [row(A_WIDTH), row(A_WIDTH), row(B_WIDTH), row(C_WIDTH), row(N_BRANCH * D_MODEL),
                    row(D_MODEL)] + weights + tail
        args = (ug, vn, bo, co, sg, x, wm, bsb, wb, wo)
    else:
        assert seq_len % tm == 0
        body = functools.partial(_merge_conv_kernel, seq_len // tm)
        per = tm // _ZPAD
        halo = pl.BlockSpec((_ZPAD, C_WIDTH), lambda i: (jnp.maximum(i * per - 1, 0), 0))
        in_specs = [row(A_WIDTH), row(A_WIDTH), row(B_WIDTH), row(C_WIDTH), halo, row(C_WIDTH),
                    row(N_BRANCH * D_MODEL), row(D_MODEL)] + weights + [
                        _const_spec((CONV_WIDTH, C_WIDTH))] + tail
        args = (ug, vn, bo, zc, zc, gb, sg, x, wm, bsb, cw, wb, wo)
        scratch = scratch + [pltpu.VMEM((_ZPAD + tm, C_WIDTH), F32)]
    return pl.pallas_call(
        body,
        grid=(t // tm,),
        in_specs=in_specs,
        out_specs=row(D_MODEL),
        out_shape=jax.ShapeDtypeStruct((t, D_MODEL), F32),
        scratch_shapes=scratch,
        compiler_params=_params("parallel"),
        name="merge",
    )(*args)


_FF_SPLITS = ((0, 1280), (1280, 1536))


def _ffn_kernel(x_ref, g_ref, w1_ref, w3_ref, w2_ref, o_ref):
    x = x_ref[...]
    h = _rms(x, g_ref[...]).astype(BF16)
    y = x
    for a, n in _FF_SPLITS:
        u = jnp.dot(h, w1_ref[:, a:a + n], preferred_element_type=F32)
        g = jnp.dot(h, w3_ref[:, a:a + n], preferred_element_type=F32)
        mid = (jax.nn.silu(u) * g).astype(BF16)
        y = y + jnp.dot(mid, w2_ref[a:a + n, :], preferred_element_type=F32)
    o_ref[...] = y


def _ffn(x, g, w1, w3, w2):
    t = x.shape[0]
    tm = TOKEN_TILE
    row = pl.BlockSpec((tm, D_MODEL), lambda i: (i, 0))
    return pl.pallas_call(
        _ffn_kernel,
        grid=(t // tm,),
        in_specs=[row, _const_spec((1, D_MODEL)), _const_spec((D_MODEL, D_FF)),
                  _const_spec((D_MODEL, D_FF)), _const_spec((D_FF, D_MODEL))],
        out_specs=row,
        out_shape=jax.ShapeDtypeStruct((t, D_MODEL), F32),
        compiler_params=_params("parallel"),
        name="ffn",
    )(x, g, w1, w3, w2)


_M_E1, _M_E2, _M_R1, _M_R2, _M_W1, _M_W2 = range(6)


def _pair_specs(n_first):
    tm = TOKEN_TILE
    return (pl.BlockSpec((tm, D_MODEL), lambda i: (jnp.minimum(i, n_first - 1), 0)),
            pl.BlockSpec((tm, D_MODEL), lambda i: (jnp.maximum(i - n_first, 0), 0)))


def _pair_tiles(xa, xb):
    tm = TOKEN_TILE
    assert xa.shape[0] % tm == 0 and xb.shape[0] % tm == 0 and xa.shape[0] > 0 and xb.shape[0] > 0
    return xa.shape[0] // tm, (xa.shape[0] + xb.shape[0]) // tm


def _router_kernel(n_first, xa_ref, xb_ref, g_ref, wrt_ref, br_ref, tri_ref, meta_ref, cnt_ref, cnt_s):
    @pl.when(pl.program_id(0) == 0)
    def _():
        cnt_s[...] = jnp.zeros_like(cnt_s)

    x = jnp.where(pl.program_id(0) < n_first, xa_ref[...], xb_ref[...])
    h = _rms(x, g_ref[...])
    logits = lax.dot_general(wrt_ref[...], h, (((1,), (1,)), ((), ())), preferred_element_type=F32,
                             precision=lax.Precision.HIGHEST) + br_ref[...]
    eid = lax.broadcasted_iota(jnp.int32, logits.shape, 0)
    m1 = jnp.max(logits, axis=0, keepdims=True)
    i1 = jnp.min(jnp.where(logits == m1, eid, N_EXPERTS), axis=0, keepdims=True)
    rest = jnp.where(eid == i1, NEG_INF, logits)
    m2 = jnp.max(rest, axis=0, keepdims=True)
    i2 = jnp.min(jnp.where(rest == m2, eid, N_EXPERTS), axis=0, keepdims=True)
    e2 = jnp.exp(m2 - m1)
    den = 1.0 + e2
    oh1 = (eid == i1).astype(F32)
    oh2 = (eid == i2).astype(F32)
    both = oh1 + oh2
    before = jnp.dot(both.astype(BF16), tri_ref[...], preferred_element_type=F32) + cnt_s[:, 0:1]
    meta_ref[_M_E1:_M_E1 + 1, :] = i1.astype(F32)
    meta_ref[_M_E2:_M_E2 + 1, :] = i2.astype(F32)
    meta_ref[_M_R1:_M_R1 + 1, :] = jnp.sum(oh1 * before, axis=0, keepdims=True)
    meta_ref[_M_R2:_M_R2 + 1, :] = jnp.sum(oh2 * before, axis=0, keepdims=True)
    meta_ref[_M_W1:_M_W1 + 1, :] = 1.0 / den
    meta_ref[_M_W2:_M_W2 + 1, :] = e2 / den
    meta_ref[_M_W2 + 1:, :] = jnp.zeros((SUBLANES - _M_W2 - 1, logits.shape[1]), F32)
    cnt_s[...] = cnt_s[...] + jnp.sum(both, axis=1, keepdims=True)
    cnt_ref[...] = cnt_s[...]


def _router(xa, xb, g, wrt, br):
    tm = TOKEN_TILE
    n_first, nt = _pair_tiles(xa, xb)
    tri = jnp.asarray(np.triu(np.ones((tm, tm), np.float32), 1), BF16)
    return pl.pallas_call(
        functools.partial(_router_kernel, n_first),
        grid=(nt,),
        in_specs=[*_pair_specs(n_first), _const_spec((1, D_MODEL)),
                  _const_spec((N_EXPERTS, D_MODEL)), _const_spec((N_EXPERTS, 1)), _const_spec((tm, tm))],
        out_specs=[pl.BlockSpec((None, SUBLANES, tm), lambda i: (i, 0, 0)),
                   pl.BlockSpec((N_EXPERTS, LANES), lambda i: (0, 0))],
        out_shape=[jax.ShapeDtypeStruct((nt, SUBLANES, tm), F32),
                   jax.ShapeDtypeStruct((N_EXPERTS, LANES), F32)],
        scratch_shapes=[pltpu.VMEM((N_EXPERTS, LANES), F32)],
        compiler_params=_params("arbitrary"),
        name="router",
    )(xa, xb, g, wrt, br, tri)


def _row_copy(src, s, dst, d, sem):
    return pltpu.make_async_copy(src.at[pl.ds(s, 1)], dst.at[pl.ds(d, 1)], sem)


def _dispatch_kernel(n_first, slots_ref, xa_ref, xb_ref, xs_in, xs_hbm, sem):
    del xs_in
    tm = TOKEN_TILE

    def scatter(x_ref):
        def issue(r, c):
            for k in range(TOP_K):
                _row_copy(x_ref, r, xs_hbm, slots_ref[0, k * tm + r], sem).start()
            return c

        lax.fori_loop(0, tm, issue, 0, unroll=8)

        def drain(r, c):
            for k in range(TOP_K):
                _row_copy(x_ref, r, xs_hbm, slots_ref[0, k * tm + r], sem).wait()
            return c

        lax.fori_loop(0, tm, drain, 0, unroll=8)

    @pl.when(pl.program_id(0) < n_first)
    def _():
        scatter(xa_ref)

    @pl.when(pl.program_id(0) >= n_first)
    def _():
        scatter(xb_ref)


def _dispatch(slots, xa, xb, n_rows):
    tm = TOKEN_TILE
    n_first, nt = _pair_tiles(xa, xb)
    xs0 = jnp.zeros((n_rows, D_MODEL), F32)
    return pl.pallas_call(
        functools.partial(_dispatch_kernel, n_first),
        grid=(nt,),
        in_specs=[pl.BlockSpec((None, 1, TOP_K * tm), lambda i: (i, 0, 0), memory_space=pltpu.SMEM),
                  *_pair_specs(n_first), pl.BlockSpec(memory_space=pl.ANY)],
        out_specs=pl.BlockSpec(memory_space=pl.ANY),
        out_shape=jax.ShapeDtypeStruct((n_rows, D_MODEL), F32),
        scratch_shapes=[pltpu.SemaphoreType.DMA(())],
        input_output_aliases={3: 0},
        compiler_params=_params("arbitrary"),
        name="dispatch",
    )(slots, xa, xb, xs0)


_E_COLS = 512


def _expert_kernel(te_ref, nu_ref, xs_ref, g_ref, w1_ref, w3_ref, w2_ref, ys_ref):
    i = pl.program_id(0)

    @pl.when(i < nu_ref[0])
    def _():
        h = _rms(xs_ref[...], g_ref[...]).astype(BF16)
        y = None
        for c in range(D_FF_EXPERT // _E_COLS):
            cols = slice(c * _E_COLS, (c + 1) * _E_COLS)
            u = jnp.dot(h, w1_ref[:, cols], preferred_element_type=F32)
            g = jnp.dot(h, w3_ref[:, cols], preferred_element_type=F32)
            mid = (jax.nn.silu(u) * g).astype(BF16)
            part = jnp.dot(mid, w2_ref[cols, :], preferred_element_type=F32)
            y = part if y is None else y + part
        ys_ref[...] = y

    @pl.when(i >= nu_ref[0])
    def _():
        ys_ref[...] = jnp.zeros_like(ys_ref)


def _experts(tile_expert, n_used, xs, g, w1, w3, w2):
    n_rows = xs.shape[0]
    tm = EXPERT_TILE
    row = pl.BlockSpec((tm, D_MODEL), lambda i, te, nu: (i, 0))
    wspec = lambda r, c: pl.BlockSpec((None, r, c), lambda i, te, nu: (te[i], 0, 0),
                                      pipeline_mode=pl.Buffered(1))
    return pl.pallas_call(
        _expert_kernel,
        grid_spec=pltpu.PrefetchScalarGridSpec(
            num_scalar_prefetch=2,
            grid=(n_rows // tm,),
            in_specs=[row, pl.BlockSpec((1, D_MODEL), lambda i, te, nu: (0, 0),
                                        pipeline_mode=pl.Buffered(1)),
                      wspec(D_MODEL, D_FF_EXPERT), wspec(D_MODEL, D_FF_EXPERT),
                      wspec(D_FF_EXPERT, D_MODEL)],
            out_specs=row,
        ),
        out_shape=jax.ShapeDtypeStruct((n_rows, D_MODEL), F32),
        compiler_params=_params("arbitrary"),
        name="experts",
    )(tile_expert, n_used, xs, g, w1, w3, w2)


def _combine_kernel(n_first, slots_ref, meta_ref, xa_ref, xb_ref, ys_hbm, gf_ref, oa_ref, ob_ref,
                    rows, sem):
    tm = TOKEN_TILE

    def issue(r, c):
        for k in range(TOP_K):
            _row_copy(ys_hbm, slots_ref[0, k * tm + r], rows.at[k], r, sem).start()
        return c

    lax.fori_loop(0, tm, issue, 0, unroll=8)

    def drain(r, c):
        for k in range(TOP_K):
            _row_copy(ys_hbm, slots_ref[0, k * tm + r], rows.at[k], r, sem).wait()
        return c

    lax.fori_loop(0, tm, drain, 0, unroll=8)

    meta = jnp.concatenate([meta_ref[...], jnp.zeros((LANES - SUBLANES, tm), F32)], axis=0)
    mt = meta.T
    w1 = mt[:, _M_W1:_M_W1 + 1]
    w2 = mt[:, _M_W2:_M_W2 + 1]
    i = pl.program_id(0)
    x = jnp.where(i < n_first, xa_ref[...], xb_ref[...])
    y = _rms(x + w1 * rows[0] + w2 * rows[1], gf_ref[...])

    @pl.when(i < n_first)
    def _():
        oa_ref[...] = y

    @pl.when(i >= n_first)
    def _():
        ob_ref[...] = y


def _combine(slots, meta, xa, xb, ys, gf):
    tm = TOKEN_TILE
    n_first, nt = _pair_tiles(xa, xb)
    return pl.pallas_call(
        functools.partial(_combine_kernel, n_first),
        grid=(nt,),
        in_specs=[pl.BlockSpec((None, 1, TOP_K * tm), lambda i: (i, 0, 0), memory_space=pltpu.SMEM),
                  pl.BlockSpec((None, SUBLANES, tm), lambda i: (i, 0, 0)),
                  *_pair_specs(n_first), pl.BlockSpec(memory_space=pl.ANY), _const_spec((1, D_MODEL))],
        out_specs=list(_pair_specs(n_first)),
        out_shape=[jax.ShapeDtypeStruct(xa.shape, F32), jax.ShapeDtypeStruct(xb.shape, F32)],
        scratch_shapes=[pltpu.VMEM((TOP_K, tm, D_MODEL), F32), pltpu.SemaphoreType.DMA(())],
        compiler_params=_params("arbitrary"),
        name="combine",
    )(slots, meta, xa, xb, ys, gf)


def _moe_final(xa, xb, g, wr, br, w1, w3, w2, gf):
    t = xa.shape[0] + xb.shape[0]
    tm = EXPERT_TILE
    n_tiles = (TOP_K * t + N_EXPERTS * (tm - 1) + tm - 1) // tm
    meta, cnt = _router(xa, xb, g, wr.T, br.reshape(N_EXPERTS, 1))

    counts = cnt[:, 0].astype(jnp.int32)
    tiles = (counts + tm - 1) // tm
    tile_end = jnp.cumsum(tiles)
    row_off = (tile_end - tiles) * tm
    n_used = tile_end[-1]
    tid = jnp.minimum(jnp.arange(n_tiles, dtype=jnp.int32), n_used - 1)
    tile_expert = jnp.sum(tid[:, None] >= tile_end[None, :], axis=1).astype(jnp.int32)

    def slot(e_row, r_row):
        e = meta[:, e_row, :].astype(jnp.int32)
        off = jnp.sum(jnp.where(e[..., None] == jnp.arange(N_EXPERTS), row_off, 0), axis=-1)
        return off + meta[:, r_row, :].astype(jnp.int32)

    slots = jnp.concatenate([slot(_M_E1, _M_R1), slot(_M_E2, _M_R2)], axis=1)[:, None, :]

    xs = _dispatch(slots, xa, xb, n_tiles * tm)
    ys = _experts(tile_expert, n_used.reshape(1), xs, g, w1, w3, w2)
    return _combine(slots, meta, xa, xb, ys, gf)


def _rel_bias_t(table, n_q, n_k, offset):
    h = table.shape[0]
    span = n_q + n_k - 1
    dist = offset + np.arange(span) - (n_k - 1)
    vec = table[:, np.clip(dist, -REL_CLIP, REL_CLIP) + REL_CLIP].astype(F32)
    wide = jnp.broadcast_to(jnp.pad(vec, ((0, 0), (0, 1)))[:, None, :], (h, n_k, span + 1))
    skew = wide.reshape(h, n_k * (span + 1))[:, :n_k * span].reshape(h, n_k, span)
    return skew[:, :, n_k - 1:n_k - 1 + n_q]


_BIAS_ROWS = 128


def _prompt_bias_kernel(vec_ref, o_ref):
    n_k, n_q = o_ref.shape
    width = vec_ref.shape[-1]
    for r0 in range(0, n_k, _BIAS_ROWS):
        x = jnp.broadcast_to(vec_ref[...], (_BIAS_ROWS, width))
        x = pltpu.roll(x, (r0 - (n_k - 1)) % width, 1, stride=1, stride_axis=0)[:, :n_q]
        kc = (r0 + lax.broadcasted_iota(jnp.int32, (_BIAS_ROWS, n_q), 0)) // CHUNK
        qc = lax.broadcasted_iota(jnp.int32, (_BIAS_ROWS, n_q), 1) // CHUNK
        x = jnp.where(kc >= qc, x, NEG_INF)
        o_ref[r0:r0 + _BIAS_ROWS, :] = jnp.where(kc <= qc + REACH // CHUNK, x, NEG_INF)


def _prompt_bias(table):
    tq = ATT_TQ
    n_k = REACH + tq
    h = table.shape[0]
    span = tq + n_k - 1
    width = -(-span // LANES) * LANES
    dist = REACH + np.arange(span) - (n_k - 1)
    vec = table[:, np.clip(dist, -REL_CLIP, REL_CLIP) + REL_CLIP].astype(F32)
    vec = jnp.pad(vec, ((0, 0), (0, width - span))).reshape(h, 1, width)
    return pl.pallas_call(
        _prompt_bias_kernel,
        grid=(h,),
        in_specs=[pl.BlockSpec((None, 1, width), lambda i: (i, 0, 0))],
        out_specs=pl.BlockSpec((None, n_k, tq), lambda i: (i, 0, 0)),
        out_shape=jax.ShapeDtypeStruct((h, n_k, tq), F32),
        compiler_params=_params("parallel"),
        name="prompt_bias",
    )(vec)


def _gmlp_mats(ws, bs, lc):
    pos = np.arange(lc)
    mask = (pos[None, :] // CHUNK) <= (pos[:, None] // CHUNK)
    w = jnp.where(mask[None], ws[:, :lc, :lc], 0)
    reps = A_CHUNK // lc
    if reps > 1:
        eye = jnp.eye(reps, dtype=w.dtype)
        w = jnp.einsum('rs,gij->grisj', eye, w).reshape(A_GROUPS, A_CHUNK, A_CHUNK)
    b = jnp.tile(bs[:, :lc], (1, reps))
    bsb = jnp.broadcast_to(b[:, :, None], (A_GROUPS, A_CHUNK, A_GROUP_DIM))
    return w.astype(BF16), bsb.astype(F32)


def kernel(x_prompt, x_sample, cache_attn_k, cache_attn_v, state_conv, norm_mix_g, w_in, gmlp_norm_g, gmlp_ws, gmlp_bs, attn_rel_bias, conv_w, w_branch, w_out, norm_ffn_g, ffn_w1, ffn_w3, ffn_w2, moe_router, moe_router_b, moe_w1, moe_w3, moe_w2, final_norm_g):
    nb, s, d = x_prompt.shape
    nbs, t, _ = x_sample.shape
    keep = cache_attn_k.shape[2]
    kv_keep = min(REACH, s)
    xp = x_prompt.reshape(nb * s, d)
    xs = x_sample.reshape(nbs * t, d)
    gf = final_norm_g.reshape(1, d)

    pk, pv, pc, sk, sv, sc, sg_out = [], [], [], [], [], [], []
    for l in range(DEPTH):
        g_mix = norm_mix_g[l].reshape(1, d)
        g_v = gmlp_norm_g[l].reshape(1, A_WIDTH)
        w_in_b = w_in[l].astype(BF16)
        wb_b = w_branch[l].astype(BF16)
        wo_b = w_out[l].astype(BF16)
        cw = conv_w[l]
        bias_p = _prompt_bias(attn_rel_bias[l])
        bias_s = jnp.swapaxes(_rel_bias_t(attn_rel_bias[l], t, keep + t, keep), 1, 2)
        wm_p, bsb_p = _gmlp_mats(gmlp_ws[l], gmlp_bs[l], A_CHUNK)
        wm_s, bsb_s = _gmlp_mats(gmlp_ws[l], gmlp_bs[l], t)

        ug, vn, q, k, v, zc, gb, sgt = _inproj(xp, g_mix, w_in_b, g_v)
        bo = _attn_prompt(q, k, v, bias_p, nb, s)
        xp = _merge(ug, vn, bo, sgt, xp, wm_p, bsb_p, wb_b, wo_b, zc=zc, gb=gb, cw=cw, seq_len=s)
        pk.append(k.reshape(nb, s, HEADS, HEAD_DIM)[:, s - kv_keep:])
        pv.append(v.reshape(nb, s, HEADS, HEAD_DIM)[:, s - kv_keep:])
        pc.append(zc.reshape(nb, s, C_WIDTH)[:, s - (CONV_WIDTH - 1):])

        ug, vn, q, k, v, zc, gb, sgt = _inproj(xs, g_mix, w_in_b, g_v)
        ck = cache_attn_k[l].reshape(nbs, keep, B_WIDTH)
        cv = cache_attn_v[l].reshape(nbs, keep, B_WIDTH)
        bo, co, ns = _seq_sample(q, k, v, ck, cv, zc, gb, state_conv[l], bias_s, cw, nbs, t)
        xs = _merge(ug, vn, bo, sgt, xs, wm_s, bsb_s, wb_b, wo_b, co=co)
        sk.append(k.reshape(nbs, t, HEADS, HEAD_DIM))
        sv.append(v.reshape(nbs, t, HEADS, HEAD_DIM))
        sc.append(ns)
        sg_out.append(vn.reshape(nbs, t, A_WIDTH))

        g_ffn = norm_ffn_g[l].reshape(1, d)
        i = l // 2
        if l % 2 == 0:
            w1 = ffn_w1[i].astype(BF16)
            w3 = ffn_w3[i].astype(BF16)
            w2 = ffn_w2[i].astype(BF16)
            xp = _ffn(xp, g_ffn, w1, w3, w2)
            xs = _ffn(xs, g_ffn, w1, w3, w2)
        else:
            assert l == DEPTH - 1
            xp, xs = _moe_final(xp, xs, g_ffn, moe_router[i], moe_router_b[i], moe_w1[i].astype(BF16),
                                moe_w3[i].astype(BF16), moe_w2[i].astype(BF16), gf)

    y_prompt = xp.reshape(nb, s, d)
    y_sample = xs.reshape(nbs, t, d)
    return (y_prompt, y_sample, jnp.stack(pk), jnp.stack(pv), jnp.stack(pc),
            jnp.stack(sk), jnp.stack(sv), jnp.stack(sc), jnp.stack(sg_out))
```

```python
import functools

import jax
import jax.numpy as jnp
import numpy as np
from jax import lax
from jax.experimental import pallas as pl
from jax.experimental.pallas import tpu as pltpu

F32 = jnp.float32
BF16 = jnp.bfloat16

D_MODEL = 1024
DEPTH = 2
CHUNK = 64
A_CHUNK = 128
A_WIDTH = 512
A_GROUPS = 4
A_GROUP_DIM = 128
HEAD_DIM = 64
HEADS = 8
B_WIDTH = 512
REACH = 512
REL_CLIP = 128
C_WIDTH = 512
CONV_WIDTH = 3
N_BRANCH = 3
D_FF = 2816
N_EXPERTS = 8
TOP_K = 2
D_FF_EXPERT = 3584
EPS = 1e-6
NEG_INF = -1e30

_O_UA, _O_VA, _O_Q, _O_K, _O_V, _O_CIN, _O_GB, _O_GC, _O_GATES = (
    0, 512, 1024, 1536, 2048, 2560, 3072, 3584, 4096)
IN_COLS = 7168

TOKEN_TILE = 512
EXPERT_TILE = 512
ATT_TQ = 256
VMEM_LIMIT = 56 * 1024 * 1024
LANES = 128
SUBLANES = 8


def _params(*sem):
    return pltpu.CompilerParams(dimension_semantics=sem, vmem_limit_bytes=VMEM_LIMIT)


def _const_spec(shape):
    nd = len(shape)
    return pl.BlockSpec(shape, lambda *_: (0,) * nd, pipeline_mode=pl.Buffered(1))


def _rms(x, g):
    ms = jnp.mean(x * x, axis=-1, keepdims=True)
    return x * lax.rsqrt(ms + EPS) * g


def _inproj_kernel(x_ref, g_ref, w_ref, gv_ref,
                   ug_ref, vn_ref, q_ref, k_ref, v_ref, zc_ref, gb_ref, sg_ref):
    h = _rms(x_ref[...], g_ref[...]).astype(BF16)

    def col(a, n=512):
        return jnp.dot(h, w_ref[:, a:a + n], preferred_element_type=F32)

    ug_ref[...] = jax.nn.gelu(col(_O_UA)).astype(BF16)
    vn_ref[...] = _rms(jax.nn.gelu(col(_O_VA)), gv_ref[...])
    q_ref[...] = (col(_O_Q) * (HEAD_DIM ** -0.5)).astype(BF16)
    k_ref[...] = col(_O_K)
    v_ref[...] = col(_O_V)
    zc_ref[...] = col(_O_GC) * col(_O_CIN)
    gb_ref[...] = col(_O_GB).astype(BF16)
    for j in range(N_BRANCH * D_MODEL // 512):
        sg_ref[:, j * 512:(j + 1) * 512] = jax.nn.sigmoid(col(_O_GATES + j * 512)).astype(BF16)


def _inproj(x, g, w_in, g_v):
    t = x.shape[0]
    tm = TOKEN_TILE
    row = lambda n: pl.BlockSpec((tm, n), lambda i: (i, 0))
    outs = [
        jax.ShapeDtypeStruct((t, A_WIDTH), BF16),
        jax.ShapeDtypeStruct((t, A_WIDTH), F32),
        jax.ShapeDtypeStruct((t, B_WIDTH), BF16),
        jax.ShapeDtypeStruct((t, B_WIDTH), F32),
        jax.ShapeDtypeStruct((t, B_WIDTH), F32),
        jax.ShapeDtypeStruct((t, C_WIDTH), F32),
        jax.ShapeDtypeStruct((t, C_WIDTH), BF16),
        jax.ShapeDtypeStruct((t, N_BRANCH * D_MODEL), BF16),
    ]
    return pl.pallas_call(
        _inproj_kernel,
        grid=(t // tm,),
        in_specs=[row(D_MODEL), _const_spec((1, D_MODEL)), _const_spec((D_MODEL, IN_COLS)),
                  _const_spec((1, A_WIDTH))],
        out_specs=[row(A_WIDTH), row(A_WIDTH), row(B_WIDTH), row(B_WIDTH), row(B_WIDTH),
                   row(C_WIDTH), row(C_WIDTH), row(N_BRANCH * D_MODEL)],
        out_shape=outs,
        compiler_params=_params("parallel"),
        name="inproj",
    )(x, g, w_in, g_v)


_ATT_AHEAD = 3


def _attend_tile(q_ref, kb, vt, bias_ref, bo_ref, q0, kblk0, nblk, bias_r0):
    tq = ATT_TQ
    nk = nblk * tq
    lane = lax.broadcasted_iota(jnp.int32, (tq, LANES), 1)
    orow = lax.broadcasted_iota(jnp.int32, (LANES, tq), 0)
    k0 = kblk0 * tq
    if not isinstance(k0, int):
        k0 = pl.multiple_of(k0, tq)

    def scores(h):
        cs = slice(h // 2 * LANES, (h // 2 + 1) * LANES)
        qp = q_ref[pl.ds(q0, tq), cs]
        sel = (lane < HEAD_DIM) if h % 2 == 0 else (lane >= HEAD_DIM)
        qh = jnp.where(sel, qp, jnp.zeros_like(qp))
        st = lax.dot_general(kb[pl.ds(k0, nk), cs], qh, (((1,), (1,)), ((), ())),
                             preferred_element_type=F32)
        return st + bias_ref[h, bias_r0:bias_r0 + nk, :]

    ahead = [scores(h) for h in range(_ATT_AHEAD)]
    outs = []
    for h in range(HEADS):
        cs = slice(h // 2 * LANES, (h // 2 + 1) * LANES)
        st = ahead.pop(0)
        if h + _ATT_AHEAD < HEADS:
            ahead.append(scores(h + _ATT_AHEAD))
        m = jnp.max(st, axis=0, keepdims=True)
        eb = jnp.exp(st - m).astype(BF16)
        o = None
        for b in range(nblk):
            ob = jnp.dot(vt[h % 2, kblk0 + b, cs, :], eb[b * tq:(b + 1) * tq, :],
                         preferred_element_type=F32)
            o = ob if o is None else o + ob
        l = o[HEAD_DIM:HEAD_DIM + 1, :] if h % 2 == 0 else o[0:1, :]
        outs.append(o / l)
        if h % 2 == 1:
            ot = jnp.where(orow < HEAD_DIM, outs[0], outs[1])
            bo_ref[pl.ds(q0, tq), cs] = ot.T.astype(BF16)
            outs = []


def _attn_prompt_kernel(q_ref, k_ref, v_ref, bias_ref, bo_ref, kb, vt):
    s = q_ref.shape[0]
    tq = ATT_TQ
    nwin = REACH // tq + 1
    kb[...] = k_ref[...].astype(BF16)
    even_rows = lax.broadcasted_iota(jnp.int32, (B_WIDTH, tq), 0) % LANES < HEAD_DIM
    for jb in range(s // tq):
        vtr = v_ref[jb * tq:(jb + 1) * tq, :].T
        vt[0, jb] = jnp.where(even_rows, vtr, 1.0).astype(BF16)
        vt[1, jb] = jnp.where(even_rows, 1.0, vtr).astype(BF16)
    for t in range(nwin - 1):
        _attend_tile(q_ref, kb, vt, bias_ref, bo_ref, t * tq, 0, t + 1, (nwin - 1 - t) * tq)

    def tile(t, carry):
        _attend_tile(q_ref, kb, vt, bias_ref, bo_ref, pl.multiple_of(t * tq, tq),
                     t - (nwin - 1), nwin, 0)
        return carry

    lax.fori_loop(nwin - 1, s // tq, tile, 0)


def _attn_prompt(q, k, v, bias, nb, s):
    seq = lambda n: pl.BlockSpec((s, n), lambda b: (b, 0))
    tq = ATT_TQ
    return pl.pallas_call(
        _attn_prompt_kernel,
        grid=(nb,),
        in_specs=[seq(B_WIDTH), seq(B_WIDTH), seq(B_WIDTH),
                  _const_spec((HEADS, REACH + tq, tq))],
        out_specs=seq(B_WIDTH),
        out_shape=jax.ShapeDtypeStruct((nb * s, B_WIDTH), BF16),
        scratch_shapes=[pltpu.VMEM((s, B_WIDTH), BF16),
                        pltpu.VMEM((2, s // tq, B_WIDTH, tq), BF16)],
        compiler_params=_params("parallel"),
        name="attn_prompt",
    )(q, k, v, bias)


_ZPAD = 8


def _conv_rows(zext_ref, gb, cw_ref, n):
    z0 = zext_ref[pl.ds(_ZPAD, n), :]
    z1 = zext_ref[pl.ds(_ZPAD - 1, n), :]
    z2 = zext_ref[pl.ds(_ZPAD - 2, n), :]
    y = z2 * cw_ref[0:1, :] + z1 * cw_ref[1:2, :] + z0 * cw_ref[2:3, :]
    return (gb.astype(F32) * y).astype(BF16)


def _seq_sample_kernel(q_ref, k_ref, v_ref, ck_ref, cv_ref, zc_ref, gb_ref, cs_ref, bias_ref, cw_ref,
                       bo_ref, co_ref, ns_ref, kk, vv, zext):
    t = q_ref.shape[0]
    keep = ck_ref.shape[0]
    kk[0:keep, :] = jnp.concatenate([ck_ref[:, h, :] for h in range(HEADS)], axis=-1).astype(BF16)
    vv[0:keep, :] = jnp.concatenate([cv_ref[:, h, :] for h in range(HEADS)], axis=-1).astype(BF16)
    kk[keep:, :] = k_ref[...].astype(BF16)
    vv[keep:, :] = v_ref[...].astype(BF16)
    hist = CONV_WIDTH - 1
    zext[_ZPAD - hist:_ZPAD, :] = cs_ref[...]
    zext[_ZPAD:, :] = zc_ref[...]

    lane = lax.broadcasted_iota(jnp.int32, (t, LANES), 1)
    for p in range(HEADS // 2):
        cs = slice(p * LANES, (p + 1) * LANES)
        qp = q_ref[:, cs]
        kp = kk[:, cs]
        vp = vv[:, cs]
        outs = []
        for hh in range(2):
            sel = (lane < HEAD_DIM) if hh == 0 else (lane >= HEAD_DIM)
            qh = jnp.where(sel, qp, jnp.zeros_like(qp))
            s = lax.dot_general(qh, kp, (((1,), (1,)), ((), ())), preferred_element_type=F32)
            s = s + bias_ref[2 * p + hh]
            m = jnp.max(s, axis=-1, keepdims=True)
            e = jnp.exp(s - m)
            l = jnp.sum(e, axis=-1, keepdims=True)
            outs.append(jnp.dot(e.astype(BF16), vp, preferred_element_type=F32) / l)
        bo_ref[:, cs] = jnp.where(lane < HEAD_DIM, outs[0], outs[1]).astype(BF16)

    co_ref[...] = _conv_rows(zext, gb_ref[...], cw_ref, t)
    ns_ref[...] = zext[_ZPAD + t - hist:_ZPAD + t, :]


def _seq_sample(q, k, v, ck, cv, layer, zc, gb, cs, bias, cw, nb, t):
    keep = ck.shape[2]
    hist = CONV_WIDTH - 1
    seq = lambda n: pl.BlockSpec((t, n), lambda b: (b, 0))
    per_b = lambda r, n: pl.BlockSpec((None, r, n), lambda b: (b, 0, 0))
    cache = pl.BlockSpec((None, None, keep, HEADS, HEAD_DIM), lambda b: (layer, b, 0, 0, 0))
    return pl.pallas_call(
        _seq_sample_kernel,
        grid=(nb,),
        in_specs=[seq(B_WIDTH), seq(B_WIDTH), seq(B_WIDTH), cache, cache,
                  seq(C_WIDTH), seq(C_WIDTH), per_b(hist, C_WIDTH),
                  _const_spec((HEADS, t, keep + t)), _const_spec((CONV_WIDTH, C_WIDTH))],
        out_specs=[seq(B_WIDTH), seq(C_WIDTH), per_b(hist, C_WIDTH)],
        out_shape=[jax.ShapeDtypeStruct((nb * t, B_WIDTH), BF16),
                   jax.ShapeDtypeStruct((nb * t, C_WIDTH), BF16),
                   jax.ShapeDtypeStruct((nb, hist, C_WIDTH), F32)],
        scratch_shapes=[pltpu.VMEM((keep + t, B_WIDTH), BF16),
                        pltpu.VMEM((keep + t, B_WIDTH), BF16),
                        pltpu.VMEM((_ZPAD + t, C_WIDTH), F32)],
        compiler_params=_params("parallel"),
        name="seq_sample",
    )(q, k, v, ck, cv, zc, gb, cs, bias, cw)


def _merge_body(ug_ref, vn_ref, bo_ref, co, sg_ref, x_ref, wm_ref, bsb_ref, wb_ref, wo_ref, o_ref, a_s):
    tm = ug_ref.shape[0]
    for n in range(tm // A_CHUNK):
        rows = slice(n * A_CHUNK, (n + 1) * A_CHUNK)
        for g in range(A_GROUPS):
            cols = slice(g * A_GROUP_DIM, (g + 1) * A_GROUP_DIM)
            mixed = jnp.dot(wm_ref[g], vn_ref[rows, cols].astype(BF16),
                            preferred_element_type=F32) + bsb_ref[g]
            a_s[rows, cols] = (ug_ref[rows, cols].astype(F32) * mixed).astype(BF16)
    m = None
    for n, br in enumerate((a_s[...], bo_ref[...], co)):
        p = jnp.dot(br, wb_ref[n], preferred_element_type=F32)
        term = sg_ref[:, n * D_MODEL:(n + 1) * D_MODEL].astype(F32) * p
        m = term if m is None else m + term
    y = jnp.dot(m.astype(BF16), wo_ref[...], preferred_element_type=F32)
    o_ref[...] = x_ref[...] + y


def _merge_conv_kernel(tiles_per_seq, ug_ref, vn_ref, bo_ref, zc_ref, halo_ref, gb_ref, sg_ref, x_ref,
                       wm_ref, bsb_ref, cw_ref, wb_ref, wo_ref, o_ref, a_s, zext):
    tm = ug_ref.shape[0]
    first = pl.program_id(0) % tiles_per_seq == 0
    zext[0:_ZPAD, :] = jnp.where(first, 0.0, halo_ref[...])
    zext[_ZPAD:, :] = zc_ref[...]
    co = _conv_rows(zext, gb_ref[...], cw_ref, tm)
    _merge_body(ug_ref, vn_ref, bo_ref, co, sg_ref, x_ref, wm_ref, bsb_ref, wb_ref, wo_ref, o_ref, a_s)


def _merge_kernel(ug_ref, vn_ref, bo_ref, co_ref, sg_ref, x_ref, wm_ref, bsb_ref, wb_ref, wo_ref,
                  o_ref, a_s):
    _merge_body(ug_ref, vn_ref, bo_ref, co_ref[...], sg_ref, x_ref, wm_ref, bsb_ref, wb_ref, wo_ref,
                o_ref, a_s)


def _merge(ug, vn, bo, sg, x, wm, bsb, wb, wo, *, co=None, zc=None, gb=None, cw=None, seq_len=None):
    t = x.shape[0]
    tm = TOKEN_TILE
    row = lambda n: pl.BlockSpec((tm, n), lambda i: (i, 0))
    weights = [_const_spec((A_GROUPS, A_CHUNK, A_CHUNK)), _const_spec((A_GROUPS, A_CHUNK, A_GROUP_DIM))]
    tail = [_const_spec((N_BRANCH, A_WIDTH, D_MODEL)), _const_spec((D_MODEL, D_MODEL))]
    scratch = [pltpu.VMEM((tm, A_WIDTH), BF16)]
    if co is not None:
        body = _merge_kernel
        in_specs = [row(A_WIDTH), row(A_WIDTH), row(B_WIDTH), row(C_WIDTH), row(N_BRANCH * D_MODEL),
                    row(D_MODEL)] + weights + tail
        args = (ug, vn, bo, co, sg, x, wm, bsb, wb, wo)
    else:
        assert seq_len % tm == 0
        body = functools.partial(_merge_conv_kernel, seq_len // tm)
        per = tm // _ZPAD
        halo = pl.BlockSpec((_ZPAD, C_WIDTH), lambda i: (jnp.maximum(i * per - 1, 0), 0))
        in_specs = [row(A_WIDTH), row(A_WIDTH), row(B_WIDTH), row(C_WIDTH), halo, row(C_WIDTH),
                    row(N_BRANCH * D_MODEL), row(D_MODEL)] + weights + [
                        _const_spec((CONV_WIDTH, C_WIDTH))] + tail
        args = (ug, vn, bo, zc, zc, gb, sg, x, wm, bsb, cw, wb, wo)
        scratch = scratch + [pltpu.VMEM((_ZPAD + tm, C_WIDTH), F32)]
    return pl.pallas_call(
        body,
        grid=(t // tm,),
        in_specs=in_specs,
        out_specs=row(D_MODEL),
        out_shape=jax.ShapeDtypeStruct((t, D_MODEL), F32),
        scratch_shapes=scratch,
        compiler_params=_params("parallel"),
        name="merge",
    )(*args)


_FF_SPLITS = ((0, 1280), (1280, 1536))


def _ffn_kernel(x_ref, g_ref, w1_ref, w3_ref, w2_ref, o_ref):
    x = x_ref[...]
    h = _rms(x, g_ref[...]).astype(BF16)
    y = x
    for a, n in _FF_SPLITS:
        u = jnp.dot(h, w1_ref[:, a:a + n], preferred_element_type=F32)
        g = jnp.dot(h, w3_ref[:, a:a + n], preferred_element_type=F32)
        mid = (jax.nn.silu(u) * g).astype(BF16)
        y = y + jnp.dot(mid, w2_ref[a:a + n, :], preferred_element_type=F32)
    o_ref[...] = y


def _ffn(x, g, w1, w3, w2):
    t = x.shape[0]
    tm = TOKEN_TILE
    row = pl.BlockSpec((tm, D_MODEL), lambda i: (i, 0))
    return pl.pallas_call(
        _ffn_kernel,
        grid=(t // tm,),
        in_specs=[row, _const_spec((1, D_MODEL)), _const_spec((D_MODEL, D_FF)),
                  _const_spec((D_MODEL, D_FF)), _const_spec((D_FF, D_MODEL))],
        out_specs=row,
        out_shape=jax.ShapeDtypeStruct((t, D_MODEL), F32),
        compiler_params=_params("parallel"),
        name="ffn",
    )(x, g, w1, w3, w2)


_M_E1, _M_E2, _M_R1, _M_R2, _M_W1, _M_W2 = range(6)


def _pair_specs(n_first):
    tm = TOKEN_TILE
    return (pl.BlockSpec((tm, D_MODEL), lambda i: (jnp.minimum(i, n_first - 1), 0)),
            pl.BlockSpec((tm, D_MODEL), lambda i: (jnp.maximum(i - n_first, 0), 0)))


def _pair_tiles(xa, xb):
    tm = TOKEN_TILE
    assert xa.shape[0] % tm == 0 and xb.shape[0] % tm == 0 and xa.shape[0] > 0 and xb.shape[0] > 0
    return xa.shape[0] // tm, (xa.shape[0] + xb.shape[0]) // tm


def _router_kernel(n_first, xa_ref, xb_ref, g_ref, wrt_ref, br_ref, tri_ref, meta_ref, cnt_ref, cnt_s):
    @pl.when(pl.program_id(0) == 0)
    def _():
        cnt_s[...] = jnp.zeros_like(cnt_s)

    x = jnp.where(pl.program_id(0) < n_first, xa_ref[...], xb_ref[...])
    h = _rms(x, g_ref[...])
    logits = lax.dot_general(wrt_ref[...], h, (((1,), (1,)), ((), ())), preferred_element_type=F32,
                             precision=lax.Precision.HIGHEST) + br_ref[...]
    eid = lax.broadcasted_iota(jnp.int32, logits.shape, 0)
    m1 = jnp.max(logits, axis=0, keepdims=True)
    i1 = jnp.min(jnp.where(logits == m1, eid, N_EXPERTS), axis=0, keepdims=True)
    rest = jnp.where(eid == i1, NEG_INF, logits)
    m2 = jnp.max(rest, axis=0, keepdims=True)
    i2 = jnp.min(jnp.where(rest == m2, eid, N_EXPERTS), axis=0, keepdims=True)
    e2 = jnp.exp(m2 - m1)
    den = 1.0 + e2
    oh1 = (eid == i1).astype(F32)
    oh2 = (eid == i2).astype(F32)
    both = oh1 + oh2
    before = jnp.dot(both.astype(BF16), tri_ref[...], preferred_element_type=F32) + cnt_s[:, 0:1]
    meta_ref[_M_E1:_M_E1 + 1, :] = i1.astype(F32)
    meta_ref[_M_E2:_M_E2 + 1, :] = i2.astype(F32)
    meta_ref[_M_R1:_M_R1 + 1, :] = jnp.sum(oh1 * before, axis=0, keepdims=True)
    meta_ref[_M_R2:_M_R2 + 1, :] = jnp.sum(oh2 * before, axis=0, keepdims=True)
    meta_ref[_M_W1:_M_W1 + 1, :] = 1.0 / den
    meta_ref[_M_W2:_M_W2 + 1, :] = e2 / den
    meta_ref[_M_W2 + 1:, :] = jnp.zeros((SUBLANES - _M_W2 - 1, logits.shape[1]), F32)
    cnt_s[...] = cnt_s[...] + jnp.sum(both, axis=1, keepdims=True)
    cnt_ref[...] = cnt_s[...]


def _router(xa, xb, g, wrt, br):
    tm = TOKEN_TILE
    n_first, nt = _pair_tiles(xa, xb)
    tri = jnp.asarray(np.triu(np.ones((tm, tm), np.float32), 1), BF16)
    return pl.pallas_call(
        functools.partial(_router_kernel, n_first),
        grid=(nt,),
        in_specs=[*_pair_specs(n_first), _const_spec((1, D_MODEL)),
                  _const_spec((N_EXPERTS, D_MODEL)), _const_spec((N_EXPERTS, 1)), _const_spec((tm, tm))],
        out_specs=[pl.BlockSpec((None, SUBLANES, tm), lambda i: (i, 0, 0)),
                   pl.BlockSpec((N_EXPERTS, LANES), lambda i: (0, 0))],
        out_shape=[jax.ShapeDtypeStruct((nt, SUBLANES, tm), F32),
                   jax.ShapeDtypeStruct((N_EXPERTS, LANES), F32)],
        scratch_shapes=[pltpu.VMEM((N_EXPERTS, LANES), F32)],
        compiler_params=_params("arbitrary"),
        name="router",
    )(xa, xb, g, wrt, br, tri)


def _row_copy(src, s, dst, d, sem):
    return pltpu.make_async_copy(src.at[pl.ds(s, 1)], dst.at[pl.ds(d, 1)], sem)


def _dispatch_kernel(n_first, slots_ref, xa_ref, xb_ref, xs_in, xs_hbm, sem):
    del xs_in
    tm = TOKEN_TILE

    def scatter(x_ref):
        def issue(r, c):
            for k in range(TOP_K):
                _row_copy(x_ref, r, xs_hbm, slots_ref[0, k * tm + r], sem).start()
            return c

        lax.fori_loop(0, tm, issue, 0, unroll=8)

        def drain(r, c):
            for k in range(TOP_K):
                _row_copy(x_ref, r, xs_hbm, slots_ref[0, k * tm + r], sem).wait()
            return c

        lax.fori_loop(0, tm, drain, 0, unroll=8)

    @pl.when(pl.program_id(0) < n_first)
    def _():
        scatter(xa_ref)

    @pl.when(pl.program_id(0) >= n_first)
    def _():
        scatter(xb_ref)


def _dispatch(slots, xa, xb, n_rows):
    tm = TOKEN_TILE
    n_first, nt = _pair_tiles(xa, xb)
    xs0 = jnp.zeros((n_rows, D_MODEL), F32)
    return pl.pallas_call(
        functools.partial(_dispatch_kernel, n_first),
        grid=(nt,),
        in_specs=[pl.BlockSpec((None, 1, TOP_K * tm), lambda i: (i, 0, 0), memory_space=pltpu.SMEM),
                  *_pair_specs(n_first), pl.BlockSpec(memory_space=pl.ANY)],
        out_specs=pl.BlockSpec(memory_space=pl.ANY),
        out_shape=jax.ShapeDtypeStruct((n_rows, D_MODEL), F32),
        scratch_shapes=[pltpu.SemaphoreType.DMA(())],
        input_output_aliases={3: 0},
        compiler_params=_params("arbitrary"),
        name="dispatch",
    )(slots, xa, xb, xs0)


_E_COLS = 512


def _expert_kernel(te_ref, nu_ref, xs_ref, g_ref, w1_ref, w3_ref, w2_ref, ys_ref):
    i = pl.program_id(0)

    @pl.when(i < nu_ref[0])
    def _():
        h = _rms(xs_ref[...], g_ref[...]).astype(BF16)
        y = None
        for c in range(D_FF_EXPERT // _E_COLS):
            cols = slice(c * _E_COLS, (c + 1) * _E_COLS)
            u = jnp.dot(h, w1_ref[:, cols], preferred_element_type=F32)
            g = jnp.dot(h, w3_ref[:, cols], preferred_element_type=F32)
            mid = (jax.nn.silu(u) * g).astype(BF16)
            part = jnp.dot(mid, w2_ref[cols, :], preferred_element_type=F32)
            y = part if y is None else y + part
        ys_ref[...] = y

    @pl.when(i >= nu_ref[0])
    def _():
        ys_ref[...] = jnp.zeros_like(ys_ref)


def _experts(tile_expert, n_used, xs, g, w1, w3, w2):
    n_rows = xs.shape[0]
    tm = EXPERT_TILE
    row = pl.BlockSpec((tm, D_MODEL), lambda i, te, nu: (i, 0))
    wspec = lambda r, c: pl.BlockSpec((None, r, c), lambda i, te, nu: (te[i], 0, 0),
                                      pipeline_mode=pl.Buffered(1))
    return pl.pallas_call(
        _expert_kernel,
        grid_spec=pltpu.PrefetchScalarGridSpec(
            num_scalar_prefetch=2,
            grid=(n_rows // tm,),
            in_specs=[row, pl.BlockSpec((1, D_MODEL), lambda i, te, nu: (0, 0),
                                        pipeline_mode=pl.Buffered(1)),
                      wspec(D_MODEL, D_FF_EXPERT), wspec(D_MODEL, D_FF_EXPERT),
                      wspec(D_FF_EXPERT, D_MODEL)],
            out_specs=row,
        ),
        out_shape=jax.ShapeDtypeStruct((n_rows, D_MODEL), F32),
        compiler_params=_params("arbitrary"),
        name="experts",
    )(tile_expert, n_used, xs, g, w1, w3, w2)


def _combine_kernel(n_first, slots_ref, meta_ref, xa_ref, xb_ref, ys_hbm, gf_ref, oa_ref, ob_ref,
                    rows, sem):
    tm = TOKEN_TILE

    def issue(r, c):
        for k in range(TOP_K):
            _row_copy(ys_hbm, slots_ref[0, k * tm + r], rows.at[k], r, sem).start()
        return c

    lax.fori_loop(0, tm, issue, 0, unroll=8)

    def drain(r, c):
        for k in range(TOP_K):
            _row_copy(ys_hbm, slots_ref[0, k * tm + r], rows.at[k], r, sem).wait()
        return c

    lax.fori_loop(0, tm, drain, 0, unroll=8)

    meta = jnp.concatenate([meta_ref[...], jnp.zeros((LANES - SUBLANES, tm), F32)], axis=0)
    mt = meta.T
    w1 = mt[:, _M_W1:_M_W1 + 1]
    w2 = mt[:, _M_W2:_M_W2 + 1]
    i = pl.program_id(0)
    x = jnp.where(i < n_first, xa_ref[...], xb_ref[...])
    y = _rms(x + w1 * rows[0] + w2 * rows[1], gf_ref[...])

    @pl.when(i < n_first)
    def _():
        oa_ref[...] = y

    @pl.when(i >= n_first)
    def _():
        ob_ref[...] = y


def _combine(slots, meta, xa, xb, ys, gf):
    tm = TOKEN_TILE
    n_first, nt = _pair_tiles(xa, xb)
    return pl.pallas_call(
        functools.partial(_combine_kernel, n_first),
        grid=(nt,),
        in_specs=[pl.BlockSpec((None, 1, TOP_K * tm), lambda i: (i, 0, 0), memory_space=pltpu.SMEM),
                  pl.BlockSpec((None, SUBLANES, tm), lambda i: (i, 0, 0)),
                  *_pair_specs(n_first), pl.BlockSpec(memory_space=pl.ANY), _const_spec((1, D_MODEL))],
        out_specs=list(_pair_specs(n_first)),
        out_shape=[jax.ShapeDtypeStruct(xa.shape, F32), jax.ShapeDtypeStruct(xb.shape, F32)],
        scratch_shapes=[pltpu.VMEM((TOP_K, tm, D_MODEL), F32), pltpu.SemaphoreType.DMA(())],
        compiler_params=_params("arbitrary"),
        name="combine",
    )(slots, meta, xa, xb, ys, gf)


def _moe_final(xa, xb, g, wr, br, w1, w3, w2, gf):
    t = xa.shape[0] + xb.shape[0]
    tm = EXPERT_TILE
    n_tiles = (TOP_K * t + N_EXPERTS * (tm - 1) + tm - 1) // tm
    meta, cnt = _router(xa, xb, g, wr.T, br.reshape(N_EXPERTS, 1))

    counts = cnt[:, 0].astype(jnp.int32)
    tiles = (counts + tm - 1) // tm
    tile_end = jnp.cumsum(tiles)
    row_off = (tile_end - tiles) * tm
    n_used = tile_end[-1]
    tid = jnp.minimum(jnp.arange(n_tiles, dtype=jnp.int32), n_used - 1)
    tile_expert = jnp.sum(tid[:, None] >= tile_end[None, :], axis=1).astype(jnp.int32)

    def slot(e_row, r_row):
        e = meta[:, e_row, :].astype(jnp.int32)
        off = jnp.sum(jnp.where(e[..., None] == jnp.arange(N_EXPERTS), row_off, 0), axis=-1)
        return off + meta[:, r_row, :].astype(jnp.int32)

    slots = jnp.concatenate([slot(_M_E1, _M_R1), slot(_M_E2, _M_R2)], axis=1)[:, None, :]

    xs = _dispatch(slots, xa, xb, n_tiles * tm)
    ys = _experts(tile_expert, n_used.reshape(1), xs, g, w1, w3, w2)
    return _combine(slots, meta, xa, xb, ys, gf)


def _rel_bias_t(table, n_q, n_k, offset):
    h = table.shape[0]
    span = n_q + n_k - 1
    dist = offset + np.arange(span) - (n_k - 1)
    vec = table[:, np.clip(dist, -REL_CLIP, REL_CLIP) + REL_CLIP].astype(F32)
    wide = jnp.broadcast_to(jnp.pad(vec, ((0, 0), (0, 1)))[:, None, :], (h, n_k, span + 1))
    skew = wide.reshape(h, n_k * (span + 1))[:, :n_k * span].reshape(h, n_k, span)
    return skew[:, :, n_k - 1:n_k - 1 + n_q]


_BIAS_ROWS = 128


def _prompt_bias_kernel(vec_ref, o_ref):
    n_k, n_q = o_ref.shape
    width = vec_ref.shape[-1]
    for r0 in range(0, n_k, _BIAS_ROWS):
        x = jnp.broadcast_to(vec_ref[...], (_BIAS_ROWS, width))
        x = pltpu.roll(x, (r0 - (n_k - 1)) % width, 1, stride=1, stride_axis=0)[:, :n_q]
        kc = (r0 + lax.broadcasted_iota(jnp.int32, (_BIAS_ROWS, n_q), 0)) // CHUNK
        qc = lax.broadcasted_iota(jnp.int32, (_BIAS_ROWS, n_q), 1) // CHUNK
        x = jnp.where(kc >= qc, x, NEG_INF)
        o_ref[r0:r0 + _BIAS_ROWS, :] = jnp.where(kc <= qc + REACH // CHUNK, x, NEG_INF)


def _prompt_bias(table):
    tq = ATT_TQ
    n_k = REACH + tq
    h = table.shape[0]
    span = tq + n_k - 1
    width = -(-span // LANES) * LANES
    dist = REACH + np.arange(span) - (n_k - 1)
    vec = table[:, np.clip(dist, -REL_CLIP, REL_CLIP) + REL_CLIP].astype(F32)
    vec = jnp.pad(vec, ((0, 0), (0, width - span))).reshape(h, 1, width)
    return pl.pallas_call(
        _prompt_bias_kernel,
        grid=(h,),
        in_specs=[pl.BlockSpec((None, 1, width), lambda i: (i, 0, 0))],
        out_specs=pl.BlockSpec((None, n_k, tq), lambda i: (i, 0, 0)),
        out_shape=jax.ShapeDtypeStruct((h, n_k, tq), F32),
        compiler_params=_params("parallel"),
        name="prompt_bias",
    )(vec)


def _gmlp_mats(ws, bs, lc):
    pos = np.arange(lc)
    mask = (pos[None, :] // CHUNK) <= (pos[:, None] // CHUNK)
    w = jnp.where(mask[None], ws[:, :lc, :lc], 0)
    reps = A_CHUNK // lc
    if reps > 1:
        eye = jnp.eye(reps, dtype=w.dtype)
        w = jnp.einsum('rs,gij->grisj', eye, w).reshape(A_GROUPS, A_CHUNK, A_CHUNK)
    b = jnp.tile(bs[:, :lc], (1, reps))
    bsb = jnp.broadcast_to(b[:, :, None], (A_GROUPS, A_CHUNK, A_GROUP_DIM))
    return w.astype(BF16), bsb.astype(F32)


def kernel(x_prompt, x_sample, cache_attn_k, cache_attn_v, state_conv, norm_mix_g, w_in, gmlp_norm_g, gmlp_ws, gmlp_bs, attn_rel_bias, conv_w, w_branch, w_out, norm_ffn_g, ffn_w1, ffn_w3, ffn_w2, moe_router, moe_router_b, moe_w1, moe_w3, moe_w2, final_norm_g):
    nb, s, d = x_prompt.shape
    nbs, t, _ = x_sample.shape
    keep = cache_attn_k.shape[2]
    kv_keep = min(REACH, s)
    xp = x_prompt.reshape(nb * s, d)
    xs = x_sample.reshape(nbs * t, d)
    gf = final_norm_g.reshape(1, d)

    pk, pv, pc, sk, sv, sc, sg_out = [], [], [], [], [], [], []
    for l in range(DEPTH):
        g_mix = norm_mix_g[l].reshape(1, d)
        g_v = gmlp_norm_g[l].reshape(1, A_WIDTH)
        w_in_b = w_in[l].astype(BF16)
        wb_b = w_branch[l].astype(BF16)
        wo_b = w_out[l].astype(BF16)
        cw = conv_w[l]
        bias_p = _prompt_bias(attn_rel_bias[l])
        bias_s = jnp.swapaxes(_rel_bias_t(attn_rel_bias[l], t, keep + t, keep), 1, 2)
        wm_p, bsb_p = _gmlp_mats(gmlp_ws[l], gmlp_bs[l], A_CHUNK)
        wm_s, bsb_s = _gmlp_mats(gmlp_ws[l], gmlp_bs[l], t)

        ug, vn, q, k, v, zc, gb, sgt = _inproj(xp, g_mix, w_in_b, g_v)
        bo = _attn_prompt(q, k, v, bias_p, nb, s)
        xp = _merge(ug, vn, bo, sgt, xp, wm_p, bsb_p, wb_b, wo_b, zc=zc, gb=gb, cw=cw, seq_len=s)
        k_new, v_new = lax.optimization_barrier((k.reshape(nb, s, B_WIDTH)[:, s - kv_keep:],
                                                 v.reshape(nb, s, B_WIDTH)[:, s - kv_keep:]))
        pk.append(k_new.reshape(nb, kv_keep, HEADS, HEAD_DIM))
        pv.append(v_new.reshape(nb, kv_keep, HEADS, HEAD_DIM))
        pc.append(zc.reshape(nb, s, C_WIDTH)[:, s - (CONV_WIDTH - 1):])

        ug, vn, q, k, v, zc, gb, sgt = _inproj(xs, g_mix, w_in_b, g_v)
        bo, co, ns = _seq_sample(q, k, v, cache_attn_k, cache_attn_v, l, zc, gb, state_conv[l], bias_s,
                                 cw, nbs, t)
        xs = _merge(ug, vn, bo, sgt, xs, wm_s, bsb_s, wb_b, wo_b, co=co)
        sk.append(k.reshape(nbs, t, HEADS, HEAD_DIM))
        sv.append(v.reshape(nbs, t, HEADS, HEAD_DIM))
        sc.append(ns)
        sg_out.append(vn.reshape(nbs, t, A_WIDTH))

        g_ffn = norm_ffn_g[l].reshape(1, d)
        i = l // 2
        if l % 2 == 0:
            w1 = ffn_w1[i].astype(BF16)
            w3 = ffn_w3[i].astype(BF16)
            w2 = ffn_w2[i].astype(BF16)
            xp = _ffn(xp, g_ffn, w1, w3, w2)
            xs = _ffn(xs, g_ffn, w1, w3, w2)
        else:
            assert l == DEPTH - 1
            xp, xs = _moe_final(xp, xs, g_ffn, moe_router[i], moe_router_b[i], moe_w1[i].astype(BF16),
                                moe_w3[i].astype(BF16), moe_w2[i].astype(BF16), gf)

    y_prompt = xp.reshape(nb, s, d)
    y_sample = xs.reshape(nbs, t, d)
    return (y_prompt, y_sample, jnp.stack(pk), jnp.stack(pv), jnp.stack(pc),
            jnp.stack(sk), jnp.stack(sv), jnp.stack(sc), jnp.stack(sg_out))
```

```python
import functools

import jax
import jax.numpy as jnp
import numpy as np
from jax import lax
from jax.experimental import pallas as pl
from jax.experimental.pallas import tpu as pltpu

F32 = jnp.float32
BF16 = jnp.bfloat16

D_MODEL = 1024
DEPTH = 2
CHUNK = 64
A_CHUNK = 128
A_WIDTH = 512
A_GROUPS = 4
A_GROUP_DIM = 128
HEAD_DIM = 64
HEADS = 8
B_WIDTH = 512
REACH = 512
REL_CLIP = 128
C_WIDTH = 512
CONV_WIDTH = 3
N_BRANCH = 3
D_FF = 2816
N_EXPERTS = 8
TOP_K = 2
D_FF_EXPERT = 3584
EPS = 1e-6
NEG_INF = -1e30

_O_UA, _O_VA, _O_Q, _O_K, _O_V, _O_CIN, _O_GB, _O_GC, _O_GATES = (
    0, 512, 1024, 1536, 2048, 2560, 3072, 3584, 4096)
IN_COLS = 7168

TOKEN_TILE = 512
EXPERT_TILE = 512
ATT_TQ = 256
VMEM_LIMIT = 56 * 1024 * 1024
LANES = 128
SUBLANES = 8


def _params(*sem):
    return pltpu.CompilerParams(dimension_semantics=sem, vmem_limit_bytes=VMEM_LIMIT)


def _const_spec(shape):
    nd = len(shape)
    return pl.BlockSpec(shape, lambda *_: (0,) * nd, pipeline_mode=pl.Buffered(1))


def _rms(x, g):
    ms = jnp.mean(x * x, axis=-1, keepdims=True)
    return x * lax.rsqrt(ms + EPS) * g


def _inproj_kernel(x_ref, g_ref, w_ref, gv_ref,
                   ug_ref, vn_ref, q_ref, k_ref, v_ref, zc_ref, gb_ref, sg_ref):
    h = _rms(x_ref[...], g_ref[...]).astype(BF16)

    def col(a, n=512):
        return jnp.dot(h, w_ref[:, a:a + n], preferred_element_type=F32)

    ug_ref[...] = jax.nn.gelu(col(_O_UA)).astype(BF16)
    vn_ref[...] = _rms(jax.nn.gelu(col(_O_VA)), gv_ref[...])
    q_ref[...] = (col(_O_Q) * (HEAD_DIM ** -0.5)).astype(BF16)
    k_ref[...] = col(_O_K)
    v_ref[...] = col(_O_V)
    zc_ref[...] = col(_O_GC) * col(_O_CIN)
    gb_ref[...] = col(_O_GB).astype(BF16)
    for j in range(N_BRANCH * D_MODEL // 512):
        sg_ref[:, j * 512:(j + 1) * 512] = jax.nn.sigmoid(col(_O_GATES + j * 512)).astype(BF16)


def _inproj(x, g, w_in, g_v):
    t = x.shape[0]
    tm = TOKEN_TILE
    row = lambda n: pl.BlockSpec((tm, n), lambda i: (i, 0))
    outs = [
        jax.ShapeDtypeStruct((t, A_WIDTH), BF16),
        jax.ShapeDtypeStruct((t, A_WIDTH), F32),
        jax.ShapeDtypeStruct((t, B_WIDTH), BF16),
        jax.ShapeDtypeStruct((t, B_WIDTH), F32),
        jax.ShapeDtypeStruct((t, B_WIDTH), F32),
        jax.ShapeDtypeStruct((t, C_WIDTH), F32),
        jax.ShapeDtypeStruct((t, C_WIDTH), BF16),
        jax.ShapeDtypeStruct((t, N_BRANCH * D_MODEL), BF16),
    ]
    return pl.pallas_call(
        _inproj_kernel,
        grid=(t // tm,),
        in_specs=[row(D_MODEL), _const_spec((1, D_MODEL)), _const_spec((D_MODEL, IN_COLS)),
                  _const_spec((1, A_WIDTH))],
        out_specs=[row(A_WIDTH), row(A_WIDTH), row(B_WIDTH), row(B_WIDTH), row(B_WIDTH),
                   row(C_WIDTH), row(C_WIDTH), row(N_BRANCH * D_MODEL)],
        out_shape=outs,
        compiler_params=_params("parallel"),
        name="inproj",
    )(x, g, w_in, g_v)


_ATT_AHEAD = 3


def _attend_tile(q_ref, kb, vt, bias_ref, bo_ref, q0, kblk0, nblk, bias_r0):
    tq = ATT_TQ
    nk = nblk * tq
    lane = lax.broadcasted_iota(jnp.int32, (tq, LANES), 1)
    orow = lax.broadcasted_iota(jnp.int32, (LANES, tq), 0)
    k0 = kblk0 * tq
    if not isinstance(k0, int):
        k0 = pl.multiple_of(k0, tq)

    def scores(h):
        cs = slice(h // 2 * LANES, (h // 2 + 1) * LANES)
        qp = q_ref[pl.ds(q0, tq), cs]
        sel = (lane < HEAD_DIM) if h % 2 == 0 else (lane >= HEAD_DIM)
        qh = jnp.where(sel, qp, jnp.zeros_like(qp))
        st = lax.dot_general(kb[pl.ds(k0, nk), cs], qh, (((1,), (1,)), ((), ())),
                             preferred_element_type=F32)
        return st + bias_ref[h, bias_r0:bias_r0 + nk, :]

    ahead = [scores(h) for h in range(_ATT_AHEAD)]
    outs = []
    for h in range(HEADS):
        cs = slice(h // 2 * LANES, (h // 2 + 1) * LANES)
        st = ahead.pop(0)
        if h + _ATT_AHEAD < HEADS:
            ahead.append(scores(h + _ATT_AHEAD))
        m = jnp.max(st, axis=0, keepdims=True)
        eb = jnp.exp(st - m).astype(BF16)
        o = None
        for b in range(nblk):
            ob = jnp.dot(vt[h % 2, kblk0 + b, cs, :], eb[b * tq:(b + 1) * tq, :],
                         preferred_element_type=F32)
            o = ob if o is None else o + ob
        l = o[HEAD_DIM:HEAD_DIM + 1, :] if h % 2 == 0 else o[0:1, :]
        outs.append(o / l)
        if h % 2 == 1:
            ot = jnp.where(orow < HEAD_DIM, outs[0], outs[1])
            bo_ref[pl.ds(q0, tq), cs] = ot.T.astype(BF16)
            outs = []


def _attn_prompt_kernel(q_ref, k_ref, v_ref, bias_ref, bo_ref, kb, vt):
    s = q_ref.shape[0]
    tq = ATT_TQ
    nwin = REACH // tq + 1
    kb[...] = k_ref[...].astype(BF16)
    even_rows = lax.broadcasted_iota(jnp.int32, (B_WIDTH, tq), 0) % LANES < HEAD_DIM
    for jb in range(s // tq):
        vtr = v_ref[jb * tq:(jb + 1) * tq, :].T
        vt[0, jb] = jnp.where(even_rows, vtr, 1.0).astype(BF16)
        vt[1, jb] = jnp.where(even_rows, 1.0, vtr).astype(BF16)
    for t in range(nwin - 1):
        _attend_tile(q_ref, kb, vt, bias_ref, bo_ref, t * tq, 0, t + 1, (nwin - 1 - t) * tq)

    def tile(t, carry):
        _attend_tile(q_ref, kb, vt, bias_ref, bo_ref, pl.multiple_of(t * tq, tq),
                     t - (nwin - 1), nwin, 0)
        return carry

    lax.fori_loop(nwin - 1, s // tq, tile, 0)


def _attn_prompt(q, k, v, bias, nb, s):
    seq = lambda n: pl.BlockSpec((s, n), lambda b: (b, 0))
    tq = ATT_TQ
    return pl.pallas_call(
        _attn_prompt_kernel,
        grid=(nb,),
        in_specs=[seq(B_WIDTH), seq(B_WIDTH), seq(B_WIDTH),
                  _const_spec((HEADS, REACH + tq, tq))],
        out_specs=seq(B_WIDTH),
        out_shape=jax.ShapeDtypeStruct((nb * s, B_WIDTH), BF16),
        scratch_shapes=[pltpu.VMEM((s, B_WIDTH), BF16),
                        pltpu.VMEM((2, s // tq, B_WIDTH, tq), BF16)],
        compiler_params=_params("parallel"),
        name="attn_prompt",
    )(q, k, v, bias)


_ZPAD = 8


def _conv_rows(zext_ref, gb, cw_ref, n):
    z0 = zext_ref[pl.ds(_ZPAD, n), :]
    z1 = zext_ref[pl.ds(_ZPAD - 1, n), :]
    z2 = zext_ref[pl.ds(_ZPAD - 2, n), :]
    y = z2 * cw_ref[0:1, :] + z1 * cw_ref[1:2, :] + z0 * cw_ref[2:3, :]
    return (gb.astype(F32) * y).astype(BF16)


def _seq_sample_kernel(q_ref, k_ref, v_ref, ck_ref, cv_ref, zc_ref, gb_ref, cs_ref, bias_ref, cw_ref,
                       bo_ref, co_ref, ns_ref, kk, vv, zext):
    t = q_ref.shape[0]
    keep = ck_ref.shape[0]
    kk[0:keep, :] = jnp.concatenate([ck_ref[:, h, :] for h in range(HEADS)], axis=-1).astype(BF16)
    vv[0:keep, :] = jnp.concatenate([cv_ref[:, h, :] for h in range(HEADS)], axis=-1).astype(BF16)
    kk[keep:, :] = k_ref[...].astype(BF16)
    vv[keep:, :] = v_ref[...].astype(BF16)
    hist = CONV_WIDTH - 1
    zext[_ZPAD - hist:_ZPAD, :] = cs_ref[...]
    zext[_ZPAD:, :] = zc_ref[...]

    lane = lax.broadcasted_iota(jnp.int32, (t, LANES), 1)
    for p in range(HEADS // 2):
        cs = slice(p * LANES, (p + 1) * LANES)
        qp = q_ref[:, cs]
        kp = kk[:, cs]
        vp = vv[:, cs]
        outs = []
        for hh in range(2):
            sel = (lane < HEAD_DIM) if hh == 0 else (lane >= HEAD_DIM)
            qh = jnp.where(sel, qp, jnp.zeros_like(qp))
            s = lax.dot_general(qh, kp, (((1,), (1,)), ((), ())), preferred_element_type=F32)
            s = s + bias_ref[2 * p + hh]
            m = jnp.max(s, axis=-1, keepdims=True)
            e = jnp.exp(s - m)
            l = jnp.sum(e, axis=-1, keepdims=True)
            outs.append(jnp.dot(e.astype(BF16), vp, preferred_element_type=F32) / l)
        bo_ref[:, cs] = jnp.where(lane < HEAD_DIM, outs[0], outs[1]).astype(BF16)

    co_ref[...] = _conv_rows(zext, gb_ref[...], cw_ref, t)
    ns_ref[...] = zext[_ZPAD + t - hist:_ZPAD + t, :]


def _seq_sample(q, k, v, ck, cv, layer, zc, gb, cs, bias, cw, nb, t):
    keep = ck.shape[2]
    hist = CONV_WIDTH - 1
    seq = lambda n: pl.BlockSpec((t, n), lambda b: (b, 0))
    per_b = lambda r, n: pl.BlockSpec((None, r, n), lambda b: (b, 0, 0))
    cache = pl.BlockSpec((None, None, keep, HEADS, HEAD_DIM), lambda b: (layer, b, 0, 0, 0))
    return pl.pallas_call(
        _seq_sample_kernel,
        grid=(nb,),
        in_specs=[seq(B_WIDTH), seq(B_WIDTH), seq(B_WIDTH), cache, cache,
                  seq(C_WIDTH), seq(C_WIDTH), per_b(hist, C_WIDTH),
                  _const_spec((HEADS, t, keep + t)), _const_spec((CONV_WIDTH, C_WIDTH))],
        out_specs=[seq(B_WIDTH), seq(C_WIDTH), per_b(hist, C_WIDTH)],
        out_shape=[jax.ShapeDtypeStruct((nb * t, B_WIDTH), BF16),
                   jax.ShapeDtypeStruct((nb * t, C_WIDTH), BF16),
                   jax.ShapeDtypeStruct((nb, hist, C_WIDTH), F32)],
        scratch_shapes=[pltpu.VMEM((keep + t, B_WIDTH), BF16),
                        pltpu.VMEM((keep + t, B_WIDTH), BF16),
                        pltpu.VMEM((_ZPAD + t, C_WIDTH), F32)],
        compiler_params=_params("parallel"),
        name="seq_sample",
    )(q, k, v, ck, cv, zc, gb, cs, bias, cw)


def _merge_body(ug_ref, vn_ref, bo_ref, co, sg_ref, x_ref, wm_ref, bsb_ref, wb_ref, wo_ref, o_ref, a_s):
    tm = ug_ref.shape[0]
    for n in range(tm // A_CHUNK):
        rows = slice(n * A_CHUNK, (n + 1) * A_CHUNK)
        for g in range(A_GROUPS):
            cols = slice(g * A_GROUP_DIM, (g + 1) * A_GROUP_DIM)
            mixed = jnp.dot(wm_ref[g], vn_ref[rows, cols].astype(BF16),
                            preferred_element_type=F32) + bsb_ref[g]
            a_s[rows, cols] = (ug_ref[rows, cols].astype(F32) * mixed).astype(BF16)
    m = None
    for n, br in enumerate((a_s[...], bo_ref[...], co)):
        p = jnp.dot(br, wb_ref[n], preferred_element_type=F32)
        term = sg_ref[:, n * D_MODEL:(n + 1) * D_MODEL].astype(F32) * p
        m = term if m is None else m + term
    y = jnp.dot(m.astype(BF16), wo_ref[...], preferred_element_type=F32)
    o_ref[...] = x_ref[...] + y


def _merge_conv_kernel(tiles_per_seq, ug_ref, vn_ref, bo_ref, zc_ref, halo_ref, gb_ref, sg_ref, x_ref,
                       wm_ref, bsb_ref, cw_ref, wb_ref, wo_ref, o_ref, a_s, zext):
    tm = ug_ref.shape[0]
    first = pl.program_id(0) % tiles_per_seq == 0
    zext[0:_ZPAD, :] = jnp.where(first, 0.0, halo_ref[...])
    zext[_ZPAD:, :] = zc_ref[...]
    co = _conv_rows(zext, gb_ref[...], cw_ref, tm)
    _merge_body(ug_ref, vn_ref, bo_ref, co, sg_ref, x_ref, wm_ref, bsb_ref, wb_ref, wo_ref, o_ref, a_s)


def _merge_kernel(ug_ref, vn_ref, bo_ref, co_ref, sg_ref, x_ref, wm_ref, bsb_ref, wb_ref, wo_ref,
                  o_ref, a_s):
    _merge_body(ug_ref, vn_ref, bo_ref, co_ref[...], sg_ref, x_ref, wm_ref, bsb_ref, wb_ref, wo_ref,
                o_ref, a_s)


def _merge(ug, vn, bo, sg, x, wm, bsb, wb, wo, *, co=None, zc=None, gb=None, cw=None, seq_len=None):
    t = x.shape[0]
    tm = TOKEN_TILE
    row = lambda n: pl.BlockSpec((tm, n), lambda i: (i, 0))
    weights = [_const_spec((A_GROUPS, A_CHUNK, A_CHUNK)), _const_spec((A_GROUPS, A_CHUNK, A_GROUP_DIM))]
    tail = [_const_spec((N_BRANCH, A_WIDTH, D_MODEL)), _const_spec((D_MODEL, D_MODEL))]
    scratch = [pltpu.VMEM((tm, A_WIDTH), BF16)]
    if co is not None:
        body = _merge_kernel
        in_specs = [row(A_WIDTH), row(A_WIDTH), row(B_WIDTH), row(C_WIDTH), row(N_BRANCH * D_MODEL),
                    row(D_MODEL)] + weights + tail
        args = (ug, vn, bo, co, sg, x, wm, bsb, wb, wo)
    else:
        assert seq_len % tm == 0
        body = functools.partial(_merge_conv_kernel, seq_len // tm)
        per = tm // _ZPAD
        halo = pl.BlockSpec((_ZPAD, C_WIDTH), lambda i: (jnp.maximum(i * per - 1, 0), 0))
        in_specs = [row(A_WIDTH), row(A_WIDTH), row(B_WIDTH), row(C_WIDTH), halo, row(C_WIDTH),
                    row(N_BRANCH * D_MODEL), row(D_MODEL)] + weights + [
                        _const_spec((CONV_WIDTH, C_WIDTH))] + tail
        args = (ug, vn, bo, zc, zc, gb, sg, x, wm, bsb, cw, wb, wo)
        scratch = scratch + [pltpu.VMEM((_ZPAD + tm, C_WIDTH), F32)]
    return pl.pallas_call(
        body,
        grid=(t // tm,),
        in_specs=in_specs,
        out_specs=row(D_MODEL),
        out_shape=jax.ShapeDtypeStruct((t, D_MODEL), F32),
        scratch_shapes=scratch,
        compiler_params=_params("parallel"),
        name="merge",
    )(*args)


_FF_SPLITS = ((0, 1280), (1280, 1536))


def _ffn_kernel(x_ref, g_ref, w1_ref, w3_ref, w2_ref, o_ref):
    x = x_ref[...]
    h = _rms(x, g_ref[...]).astype(BF16)
    y = x
    for a, n in _FF_SPLITS:
        u = jnp.dot(h, w1_ref[:, a:a + n], preferred_element_type=F32)
        g = jnp.dot(h, w3_ref[:, a:a + n], preferred_element_type=F32)
        mid = (jax.nn.silu(u) * g).astype(BF16)
        y = y + jnp.dot(mid, w2_ref[a:a + n, :], preferred_element_type=F32)
    o_ref[...] = y


def _ffn(x, g, w1, w3, w2):
    t = x.shape[0]
    tm = TOKEN_TILE
    row = pl.BlockSpec((tm, D_MODEL), lambda i: (i, 0))
    return pl.pallas_call(
        _ffn_kernel,
        grid=(t // tm,),
        in_specs=[row, _const_spec((1, D_MODEL)), _const_spec((D_MODEL, D_FF)),
                  _const_spec((D_MODEL, D_FF)), _const_spec((D_FF, D_MODEL))],
        out_specs=row,
        out_shape=jax.ShapeDtypeStruct((t, D_MODEL), F32),
        compiler_params=_params("parallel"),
        name="ffn",
    )(x, g, w1, w3, w2)


_M_E1, _M_E2, _M_R1, _M_R2, _M_W1, _M_W2 = range(6)


def _pair_specs(n_first):
    tm = TOKEN_TILE
    return (pl.BlockSpec((tm, D_MODEL), lambda i: (jnp.minimum(i, n_first - 1), 0)),
            pl.BlockSpec((tm, D_MODEL), lambda i: (jnp.maximum(i - n_first, 0), 0)))


def _pair_tiles(xa, xb):
    tm = TOKEN_TILE
    assert xa.shape[0] % tm == 0 and xb.shape[0] % tm == 0 and xa.shape[0] > 0 and xb.shape[0] > 0
    return xa.shape[0] // tm, (xa.shape[0] + xb.shape[0]) // tm


def _router_kernel(n_first, xa_ref, xb_ref, g_ref, wrt_ref, br_ref, tri_ref, meta_ref, cnt_ref, cnt_s):
    @pl.when(pl.program_id(0) == 0)
    def _():
        cnt_s[...] = jnp.zeros_like(cnt_s)

    x = jnp.where(pl.program_id(0) < n_first, xa_ref[...], xb_ref[...])
    h = _rms(x, g_ref[...])
    logits = lax.dot_general(wrt_ref[...], h, (((1,), (1,)), ((), ())), preferred_element_type=F32,
                             precision=lax.Precision.HIGHEST) + br_ref[...]
    eid = lax.broadcasted_iota(jnp.int32, logits.shape, 0)
    m1 = jnp.max(logits, axis=0, keepdims=True)
    i1 = jnp.min(jnp.where(logits == m1, eid, N_EXPERTS), axis=0, keepdims=True)
    rest = jnp.where(eid == i1, NEG_INF, logits)
    m2 = jnp.max(rest, axis=0, keepdims=True)
    i2 = jnp.min(jnp.where(rest == m2, eid, N_EXPERTS), axis=0, keepdims=True)
    e2 = jnp.exp(m2 - m1)
    den = 1.0 + e2
    oh1 = (eid == i1).astype(F32)
    oh2 = (eid == i2).astype(F32)
    both = oh1 + oh2
    before = jnp.dot(both.astype(BF16), tri_ref[...], preferred_element_type=F32) + cnt_s[:, 0:1]
    meta_ref[_M_E1:_M_E1 + 1, :] = i1.astype(F32)
    meta_ref[_M_E2:_M_E2 + 1, :] = i2.astype(F32)
    meta_ref[_M_R1:_M_R1 + 1, :] = jnp.sum(oh1 * before, axis=0, keepdims=True)
    meta_ref[_M_R2:_M_R2 + 1, :] = jnp.sum(oh2 * before, axis=0, keepdims=True)
    meta_ref[_M_W1:_M_W1 + 1, :] = 1.0 / den
    meta_ref[_M_W2:_M_W2 + 1, :] = e2 / den
    meta_ref[_M_W2 + 1:, :] = jnp.zeros((SUBLANES - _M_W2 - 1, logits.shape[1]), F32)
    cnt_s[...] = cnt_s[...] + jnp.sum(both, axis=1, keepdims=True)
    cnt_ref[...] = cnt_s[...]


def _router(xa, xb, g, wrt, br):
    tm = TOKEN_TILE
    n_first, nt = _pair_tiles(xa, xb)
    tri = jnp.asarray(np.triu(np.ones((tm, tm), np.float32), 1), BF16)
    return pl.pallas_call(
        functools.partial(_router_kernel, n_first),
        grid=(nt,),
        in_specs=[*_pair_specs(n_first), _const_spec((1, D_MODEL)),
                  _const_spec((N_EXPERTS, D_MODEL)), _const_spec((N_EXPERTS, 1)), _const_spec((tm, tm))],
        out_specs=[pl.BlockSpec((None, SUBLANES, tm), lambda i: (i, 0, 0)),
                   pl.BlockSpec((N_EXPERTS, LANES), lambda i: (0, 0))],
        out_shape=[jax.ShapeDtypeStruct((nt, SUBLANES, tm), F32),
                   jax.ShapeDtypeStruct((N_EXPERTS, LANES), F32)],
        scratch_shapes=[pltpu.VMEM((N_EXPERTS, LANES), F32)],
        compiler_params=_params("arbitrary"),
        name="router",
    )(xa, xb, g, wrt, br, tri)


def _row_copy(src, s, dst, d, sem):
    return pltpu.make_async_copy(src.at[pl.ds(s, 1)], dst.at[pl.ds(d, 1)], sem)


def _dispatch_kernel(n_first, slots_ref, xa_ref, xb_ref, xs_in, xs_hbm, sem):
    del xs_in
    tm = TOKEN_TILE

    def scatter(x_ref):
        def issue(r, c):
            for k in range(TOP_K):
                _row_copy(x_ref, r, xs_hbm, slots_ref[0, k * tm + r], sem).start(priority=k % 2)
            return c

        lax.fori_loop(0, tm, issue, 0, unroll=8)

        def drain(r, c):
            for k in range(TOP_K):
                _row_copy(x_ref, r, xs_hbm, slots_ref[0, k * tm + r], sem).wait()
            return c

        lax.fori_loop(0, tm, drain, 0, unroll=8)

    @pl.when(pl.program_id(0) < n_first)
    def _():
        scatter(xa_ref)

    @pl.when(pl.program_id(0) >= n_first)
    def _():
        scatter(xb_ref)


def _dispatch(slots, xa, xb, n_rows):
    tm = TOKEN_TILE
    n_first, nt = _pair_tiles(xa, xb)
    xs0 = jnp.zeros((n_rows, D_MODEL), F32)
    return pl.pallas_call(
        functools.partial(_dispatch_kernel, n_first),
        grid=(nt,),
        in_specs=[pl.BlockSpec((None, 1, TOP_K * tm), lambda i: (i, 0, 0), memory_space=pltpu.SMEM),
                  *_pair_specs(n_first), pl.BlockSpec(memory_space=pl.ANY)],
        out_specs=pl.BlockSpec(memory_space=pl.ANY),
        out_shape=jax.ShapeDtypeStruct((n_rows, D_MODEL), F32),
        scratch_shapes=[pltpu.SemaphoreType.DMA(())],
        input_output_aliases={3: 0},
        compiler_params=_params("arbitrary"),
        name="dispatch",
    )(slots, xa, xb, xs0)


_E_COLS = 512


def _expert_kernel(te_ref, nu_ref, xs_ref, g_ref, w1_ref, w3_ref, w2_ref, ys_ref):
    i = pl.program_id(0)

    @pl.when(i < nu_ref[0])
    def _():
        h = _rms(xs_ref[...], g_ref[...]).astype(BF16)
        y = None
        for c in range(D_FF_EXPERT // _E_COLS):
            cols = slice(c * _E_COLS, (c + 1) * _E_COLS)
            u = jnp.dot(h, w1_ref[:, cols], preferred_element_type=F32)
            g = jnp.dot(h, w3_ref[:, cols], preferred_element_type=F32)
            mid = (jax.nn.silu(u) * g).astype(BF16)
            part = jnp.dot(mid, w2_ref[cols, :], preferred_element_type=F32)
            y = part if y is None else y + part
        ys_ref[...] = y

    @pl.when(i >= nu_ref[0])
    def _():
        ys_ref[...] = jnp.zeros_like(ys_ref)


def _experts(tile_expert, n_used, xs, g, w1, w3, w2):
    n_rows = xs.shape[0]
    tm = EXPERT_TILE
    row = pl.BlockSpec((tm, D_MODEL), lambda i, te, nu: (i, 0))
    wspec = lambda r, c: pl.BlockSpec((None, r, c), lambda i, te, nu: (te[i], 0, 0),
                                      pipeline_mode=pl.Buffered(1))
    return pl.pallas_call(
        _expert_kernel,
        grid_spec=pltpu.PrefetchScalarGridSpec(
            num_scalar_prefetch=2,
            grid=(n_rows // tm,),
            in_specs=[row, pl.BlockSpec((1, D_MODEL), lambda i, te, nu: (0, 0),
                                        pipeline_mode=pl.Buffered(1)),
                      wspec(D_MODEL, D_FF_EXPERT), wspec(D_MODEL, D_FF_EXPERT),
                      wspec(D_FF_EXPERT, D_MODEL)],
            out_specs=row,
        ),
        out_shape=jax.ShapeDtypeStruct((n_rows, D_MODEL), F32),
        compiler_params=_params("arbitrary"),
        name="experts",
    )(tile_expert, n_used, xs, g, w1, w3, w2)


def _combine_kernel(n_first, slots_ref, meta_ref, xa_ref, xb_ref, ys_hbm, gf_ref, oa_ref, ob_ref,
                    rows, sem):
    tm = TOKEN_TILE

    def issue(r, c):
        for k in range(TOP_K):
            _row_copy(ys_hbm, slots_ref[0, k * tm + r], rows.at[k], r, sem).start(priority=k % 2)
        return c

    lax.fori_loop(0, tm, issue, 0, unroll=8)

    def drain(r, c):
        for k in range(TOP_K):
            _row_copy(ys_hbm, slots_ref[0, k * tm + r], rows.at[k], r, sem).wait()
        return c

    lax.fori_loop(0, tm, drain, 0, unroll=8)

    meta = jnp.concatenate([meta_ref[...], jnp.zeros((LANES - SUBLANES, tm), F32)], axis=0)
    mt = meta.T
    w1 = mt[:, _M_W1:_M_W1 + 1]
    w2 = mt[:, _M_W2:_M_W2 + 1]
    i = pl.program_id(0)
    x = jnp.where(i < n_first, xa_ref[...], xb_ref[...])
    y = _rms(x + w1 * rows[0] + w2 * rows[1], gf_ref[...])

    @pl.when(i < n_first)
    def _():
        oa_ref[...] = y

    @pl.when(i >= n_first)
    def _():
        ob_ref[...] = y


def _combine(slots, meta, xa, xb, ys, gf):
    tm = TOKEN_TILE
    n_first, nt = _pair_tiles(xa, xb)
    return pl.pallas_call(
        functools.partial(_combine_kernel, n_first),
        grid=(nt,),
        in_specs=[pl.BlockSpec((None, 1, TOP_K * tm), lambda i: (i, 0, 0), memory_space=pltpu.SMEM),
                  pl.BlockSpec((None, SUBLANES, tm), lambda i: (i, 0, 0)),
                  *_pair_specs(n_first), pl.BlockSpec(memory_space=pl.ANY), _const_spec((1, D_MODEL))],
        out_specs=list(_pair_specs(n_first)),
        out_shape=[jax.ShapeDtypeStruct(xa.shape, F32), jax.ShapeDtypeStruct(xb.shape, F32)],
        scratch_shapes=[pltpu.VMEM((TOP_K, tm, D_MODEL), F32), pltpu.SemaphoreType.DMA(())],
        compiler_params=_params("arbitrary"),
        name="combine",
    )(slots, meta, xa, xb, ys, gf)


def _moe_final(xa, xb, g, wr, br, w1, w3, w2, gf):
    t = xa.shape[0] + xb.shape[0]
    tm = EXPERT_TILE
    n_tiles = (TOP_K * t + N_EXPERTS * (tm - 1) + tm - 1) // tm
    meta, cnt = _router(xa, xb, g, wr.T, br.reshape(N_EXPERTS, 1))

    counts = cnt[:, 0].astype(jnp.int32)
    tiles = (counts + tm - 1) // tm
    tile_end = jnp.cumsum(tiles)
    row_off = (tile_end - tiles) * tm
    n_used = tile_end[-1]
    tid = jnp.minimum(jnp.arange(n_tiles, dtype=jnp.int32), n_used - 1)
    tile_expert = jnp.sum(tid[:, None] >= tile_end[None, :], axis=1).astype(jnp.int32)

    def slot(e_row, r_row):
        e = meta[:, e_row, :].astype(jnp.int32)
        off = jnp.sum(jnp.where(e[..., None] == jnp.arange(N_EXPERTS), row_off, 0), axis=-1)
        return off + meta[:, r_row, :].astype(jnp.int32)

    slots = jnp.concatenate([slot(_M_E1, _M_R1), slot(_M_E2, _M_R2)], axis=1)[:, None, :]

    xs = _dispatch(slots, xa, xb, n_tiles * tm)
    ys = _experts(tile_expert, n_used.reshape(1), xs, g, w1, w3, w2)
    return _combine(slots, meta, xa, xb, ys, gf)


def _rel_bias_t(table, n_q, n_k, offset):
    h = table.shape[0]
    span = n_q + n_k - 1
    dist = offset + np.arange(span) - (n_k - 1)
    vec = table[:, np.clip(dist, -REL_CLIP, REL_CLIP) + REL_CLIP].astype(F32)
    wide = jnp.broadcast_to(jnp.pad(vec, ((0, 0), (0, 1)))[:, None, :], (h, n_k, span + 1))
    skew = wide.reshape(h, n_k * (span + 1))[:, :n_k * span].reshape(h, n_k, span)
    return skew[:, :, n_k - 1:n_k - 1 + n_q]


_BIAS_ROWS = 128


def _prompt_bias_kernel(vec_ref, o_ref):
    n_k, n_q = o_ref.shape
    width = vec_ref.shape[-1]
    for r0 in range(0, n_k, _BIAS_ROWS):
        x = jnp.broadcast_to(vec_ref[...], (_BIAS_ROWS, width))
        x = pltpu.roll(x, (r0 - (n_k - 1)) % width, 1, stride=1, stride_axis=0)[:, :n_q]
        kc = (r0 + lax.broadcasted_iota(jnp.int32, (_BIAS_ROWS, n_q), 0)) // CHUNK
        qc = lax.broadcasted_iota(jnp.int32, (_BIAS_ROWS, n_q), 1) // CHUNK
        x = jnp.where(kc >= qc, x, NEG_INF)
        o_ref[r0:r0 + _BIAS_ROWS, :] = jnp.where(kc <= qc + REACH // CHUNK, x, NEG_INF)


def _prompt_bias(table):
    tq = ATT_TQ
    n_k = REACH + tq
    h = table.shape[0]
    span = tq + n_k - 1
    width = -(-span // LANES) * LANES
    dist = REACH + np.arange(span) - (n_k - 1)
    vec = table[:, np.clip(dist, -REL_CLIP, REL_CLIP) + REL_CLIP].astype(F32)
    vec = jnp.pad(vec, ((0, 0), (0, width - span))).reshape(h, 1, width)
    return pl.pallas_call(
        _prompt_bias_kernel,
        grid=(h,),
        in_specs=[pl.BlockSpec((None, 1, width), lambda i: (i, 0, 0))],
        out_specs=pl.BlockSpec((None, n_k, tq), lambda i: (i, 0, 0)),
        out_shape=jax.ShapeDtypeStruct((h, n_k, tq), F32),
        compiler_params=_params("parallel"),
        name="prompt_bias",
    )(vec)


def _gmlp_mats(ws, bs, lc):
    pos = np.arange(lc)
    mask = (pos[None, :] // CHUNK) <= (pos[:, None] // CHUNK)
    w = jnp.where(mask[None], ws[:, :lc, :lc], 0)
    reps = A_CHUNK // lc
    if reps > 1:
        eye = jnp.eye(reps, dtype=w.dtype)
        w = jnp.einsum('rs,gij->grisj', eye, w).reshape(A_GROUPS, A_CHUNK, A_CHUNK)
    b = jnp.tile(bs[:, :lc], (1, reps))
    bsb = jnp.broadcast_to(b[:, :, None], (A_GROUPS, A_CHUNK, A_GROUP_DIM))
    return w.astype(BF16), bsb.astype(F32)


def kernel(x_prompt, x_sample, cache_attn_k, cache_attn_v, state_conv, norm_mix_g, w_in, gmlp_norm_g, gmlp_ws, gmlp_bs, attn_rel_bias, conv_w, w_branch, w_out, norm_ffn_g, ffn_w1, ffn_w3, ffn_w2, moe_router, moe_router_b, moe_w1, moe_w3, moe_w2, final_norm_g):
    nb, s, d = x_prompt.shape
    nbs, t, _ = x_sample.shape
    keep = cache_attn_k.shape[2]
    kv_keep = min(REACH, s)
    xp = x_prompt.reshape(nb * s, d)
    xs = x_sample.reshape(nbs * t, d)
    gf = final_norm_g.reshape(1, d)

    pk, pv, pc, sk, sv, sc, sg_out = [], [], [], [], [], [], []
    for l in range(DEPTH):
        g_mix = norm_mix_g[l].reshape(1, d)
        g_v = gmlp_norm_g[l].reshape(1, A_WIDTH)
        w_in_b = w_in[l].astype(BF16)
        wb_b = w_branch[l].astype(BF16)
        wo_b = w_out[l].astype(BF16)
        cw = conv_w[l]
        bias_p = _prompt_bias(attn_rel_bias[l])
        bias_s = jnp.swapaxes(_rel_bias_t(attn_rel_bias[l], t, keep + t, keep), 1, 2)
        wm_p, bsb_p = _gmlp_mats(gmlp_ws[l], gmlp_bs[l], A_CHUNK)
        wm_s, bsb_s = _gmlp_mats(gmlp_ws[l], gmlp_bs[l], t)

        ug, vn, q, k, v, zc, gb, sgt = _inproj(xp, g_mix, w_in_b, g_v)
        bo = _attn_prompt(q, k, v, bias_p, nb, s)
        xp = _merge(ug, vn, bo, sgt, xp, wm_p, bsb_p, wb_b, wo_b, zc=zc, gb=gb, cw=cw, seq_len=s)
        k_new, v_new = lax.optimization_barrier((k.reshape(nb, s, B_WIDTH)[:, s - kv_keep:],
                                                 v.reshape(nb, s, B_WIDTH)[:, s - kv_keep:]))
        pk.append(k_new.reshape(nb, kv_keep, HEADS, HEAD_DIM))
        pv.append(v_new.reshape(nb, kv_keep, HEADS, HEAD_DIM))
        pc.append(zc.reshape(nb, s, C_WIDTH)[:, s - (CONV_WIDTH - 1):])

        ug, vn, q, k, v, zc, gb, sgt = _inproj(xs, g_mix, w_in_b, g_v)
        bo, co, ns = _seq_sample(q, k, v, cache_attn_k, cache_attn_v, l, zc, gb, state_conv[l], bias_s,
                                 cw, nbs, t)
        xs = _merge(ug, vn, bo, sgt, xs, wm_s, bsb_s, wb_b, wo_b, co=co)
        sk.append(k.reshape(nbs, t, HEADS, HEAD_DIM))
        sv.append(v.reshape(nbs, t, HEADS, HEAD_DIM))
        sc.append(ns)
        sg_out.append(vn.reshape(nbs, t, A_WIDTH))

        g_ffn = norm_ffn_g[l].reshape(1, d)
        i = l // 2
        if l % 2 == 0:
            w1 = ffn_w1[i].astype(BF16)
            w3 = ffn_w3[i].astype(BF16)
            w2 = ffn_w2[i].astype(BF16)
            xp = _ffn(xp, g_ffn, w1, w3, w2)
            xs = _ffn(xs, g_ffn, w1, w3, w2)
        else:
            assert l == DEPTH - 1
            xp, xs = _moe_final(xp, xs, g_ffn, moe_router[i], moe_router_b[i], moe_w1[i].astype(BF16),
                                moe_w3[i].astype(BF16), moe_w2[i].astype(BF16), gf)

    y_prompt = xp.reshape(nb, s, d)
    y_sample = xs.reshape(nbs, t, d)
    return (y_prompt, y_sample, jnp.stack(pk), jnp.stack(pv), jnp.stack(pc),
            jnp.stack(sk), jnp.stack(sv), jnp.stack(sc), jnp.stack(sg_out))
```
